```python
import math
import jax
import jax.numpy as jnp
from jax import lax
import numpy as np

D_MODEL = 2048
BATCH = 2
SEQ = 8192
DEPTH = 4

GRID_W = 64
CTX_LEN = 256
N_MIXERS = 2
CHUNK = 128
GM_WIDTH = 2 * D_MODEL
GM_HEADS = 16
GM_HEAD_DIM = GM_WIDTH // GM_HEADS
S5_WIDTH = D_MODEL
S5_GROUP = 16
S5_GROUPS = S5_WIDTH // S5_GROUP
S5_STATE = 64
FFN_HIDDEN = -(-8 * D_MODEL // (3 * 256)) * 256
N_GM_LAYERS = (DEPTH + 1) // 2
N_S5_LAYERS = DEPTH // 2
EPS = 1e-6

kernel_name = "hybrid_gmlp_s5_diffusion_prefix_trunk"


def rmsnorm(x, g):
    xf = x.astype(jnp.float32)
    y = xf * lax.rsqrt(jnp.mean(xf * xf, axis=-1, keepdims=True) + EPS)
    return (y * g.astype(jnp.float32)).astype(x.dtype)


def layernorm(x, g, b):
    xf = x.astype(jnp.float32)
    mu = jnp.mean(xf, axis=-1, keepdims=True)
    xc = xf - mu
    var = jnp.mean(xc * xc, axis=-1, keepdims=True)
    y = xc * lax.rsqrt(var + EPS) * g.astype(jnp.float32) + b.astype(jnp.float32)
    return y.astype(x.dtype)


def adaln(cond, w, b):
    m = jax.nn.silu(cond) @ w + b
    return jnp.split(m, 6, axis=-1)


def modulate(h, shift, scale):
    return h * (1 + scale) + shift


def sincos_2d(L):
    rows = L // GRID_W
    r, col = jnp.meshgrid(jnp.arange(rows, dtype=jnp.float32),
                          jnp.arange(GRID_W, dtype=jnp.float32), indexing='ij')
    r = r.reshape(-1)
    col = col.reshape(-1)
    q = D_MODEL // 4
    omega = 1.0 / (10000.0 ** (jnp.arange(q, dtype=jnp.float32) / q))

    def emb(p):
        ang = p[:, None] * omega[None, :]
        return jnp.concatenate([jnp.sin(ang), jnp.cos(ang)], axis=-1)

    return jnp.concatenate([emb(r), emb(col)], axis=-1)


def swiglu(h, w1, w3, w2):
    return (jax.nn.silu(h @ w1) * (h @ w3)) @ w2


def gmlp_mixer(h, w_in, ln_g, ln_b, w_s, b_s, w_out):
    bsz, L, _ = h.shape
    z = jax.nn.gelu(h @ w_in)
    u, v = jnp.split(z, 2, axis=-1)
    v = layernorm(v, ln_g, ln_b)
    n_chunks = L // CHUNK
    vc = v.reshape(bsz, n_chunks, CHUNK, GM_HEADS, GM_HEAD_DIM)
    s = jnp.einsum('hij,bcjhd->bcihd', w_s, vc) + b_s.T[None, None, :, :, None]
    return (u * s.reshape(bsz, L, GM_WIDTH)) @ w_out


def s5_discretize(a_re, a_im, log_dt, b_re, b_im):
    A = lax.complex(a_re.astype(jnp.float32), a_im.astype(jnp.float32))
    dt = jnp.exp(log_dt.astype(jnp.float32))[:, None]
    dtA = A * dt
    a_bar = jnp.exp(dtA)
    Bm = lax.complex(b_re.astype(jnp.float32), b_im.astype(jnp.float32))
    b_bar = ((a_bar - 1) / A)[..., None] * Bm
    return (dtA, a_bar, b_bar)


def _linrec_combine(e1, e2):
    a1, b1 = e1
    a2, b2 = e2
    return a1 * a2, a2 * b1 + b2


def s5_scan(u, h0, dtA, a_bar, b_bar, c_re, c_im, emit):
    bsz, L = u.shape[0], u.shape[1]
    n_chunks = L // CHUNK
    uc = jnp.moveaxis(u.reshape(bsz, n_chunks, CHUNK, S5_GROUPS, S5_GROUP), 1, 0)
    decay = jnp.exp(jnp.arange(1, CHUNK + 1, dtype=jnp.float32)[:, None, None] * dtA)
    br, bi = jnp.real(b_bar), jnp.imag(b_bar)
    cr, ci = c_re.astype(jnp.float32), c_im.astype(jnp.float32)

    def body(h, uk):
        bu = lax.complex(jnp.einsum('gpq,btgq->btgp', br, uk),
                         jnp.einsum('gpq,btgq->btgp', bi, uk))
        a = jnp.broadcast_to(a_bar, bu.shape)
        _, hs = lax.associative_scan(_linrec_combine, (a, bu), axis=1)
        hs = hs + decay * h[:, None]
        y = (jnp.einsum('gqp,btgp->btgq', cr, jnp.real(hs))
             - jnp.einsum('gqp,btgp->btgq', ci, jnp.imag(hs))) if emit else None
        return hs[:, -1], y

    h_final, ys = lax.scan(body, h0, uc)
    y = jnp.moveaxis(ys, 0, 1).reshape(bsz, L, S5_GROUPS, S5_GROUP) if emit else None
    return y, h_final


def s5_readout(y_fwd, y_bwd, u, d, w_glu, dtype):
    y = y_fwd + y_bwd + u * d.astype(jnp.float32).reshape(S5_GROUPS, S5_GROUP)
    z = jax.nn.gelu(y.reshape(y.shape[0], y.shape[1], S5_WIDTH)).astype(dtype)
    a, g = jnp.split(z @ w_glu, 2, axis=-1)
    return a * jax.nn.sigmoid(g)


def s5_mixer(hl, hc, w_in, a_re, a_im, log_dt, b_re, b_im, c_re, c_im, d, w_glu, ctx_out):
    dirs = [s5_discretize(a_re[k], a_im[k], log_dt[k], b_re[k], b_im[k]) + (c_re[k], c_im[k])
            for k in range(2)]

    def proj(h):
        return (h @ w_in).astype(jnp.float32).reshape(h.shape[0], h.shape[1], S5_GROUPS, S5_GROUP)

    uc = proj(hc)
    ul = proj(hl)
    h0 = jnp.zeros((hc.shape[0], S5_GROUPS, S5_STATE), jnp.complex64)
    y_cf, h_cf = s5_scan(uc, h0, *dirs[0], ctx_out)
    y_cb, h_cb = s5_scan(uc[:, ::-1], h0, *dirs[1], ctx_out)
    y_lf, _ = s5_scan(ul, h_cf, *dirs[0], True)
    y_lb, _ = s5_scan(ul[:, ::-1], h_cb, *dirs[1], True)
    out_l = s5_readout(y_lf, y_lb[:, ::-1], ul, d, w_glu, hl.dtype)
    out_c = s5_readout(y_cf, y_cb[:, ::-1], uc, d, w_glu, hc.dtype) if ctx_out else None
    return out_l, out_c


def setup_inputs(seed: int = 0) -> dict:
    key = jax.random.key(seed)
    ks = jax.random.split(key, 32)
    f32 = jnp.float32
    D, F, EA, EB = D_MODEL, FFN_HIDDEN, GM_WIDTH, S5_WIDTH
    G, P, Q = S5_GROUPS, S5_STATE, S5_GROUP
    nrm = lambda k, shape, s: jax.random.normal(k, shape, f32) * s
    a_im = jnp.broadcast_to(math.pi * jnp.arange(P, dtype=f32), (N_S5_LAYERS, 2, G, P))
    return {
        "x": nrm(ks[0], (BATCH, SEQ, D), 1.0),
        "c": nrm(ks[1], (BATCH, D), 1.0),
        "ctx": nrm(ks[2], (BATCH, CTX_LEN, D), 1.0),
        "c_ctx": nrm(ks[3], (D,), 1.0),
        "ada_w": nrm(ks[4], (DEPTH, D, 6 * D), 0.5 * D ** -0.5),
        "ada_b": nrm(ks[5], (DEPTH, 6 * D), 0.01),
        "norm1_g": 1.0 + nrm(ks[6], (DEPTH, D), 0.02),
        "norm2_g": 1.0 + nrm(ks[7], (DEPTH, D), 0.02),
        "ffn_w1": nrm(ks[8], (DEPTH, D, F), D ** -0.5),
        "ffn_w3": nrm(ks[9], (DEPTH, D, F), D ** -0.5),
        "ffn_w2": nrm(ks[10], (DEPTH, F, D), F ** -0.5),
        "gm_w_in": nrm(ks[11], (N_GM_LAYERS, D, 2 * EA), D ** -0.5),
        "gm_ln_g": 1.0 + nrm(ks[12], (N_GM_LAYERS, EA), 0.02),
        "gm_ln_b": nrm(ks[13], (N_GM_LAYERS, EA), 0.02),
        "gm_w_s": nrm(ks[14], (N_GM_LAYERS, GM_HEADS, CHUNK, CHUNK), CHUNK ** -0.5),
        "gm_b_s": 1.0 + nrm(ks[15], (N_GM_LAYERS, GM_HEADS, CHUNK), 0.1),
        "gm_w_out": nrm(ks[16], (N_GM_LAYERS, EA, D), EA ** -0.5),
        "s5_w_in": nrm(ks[17], (N_S5_LAYERS, D, EB), D ** -0.5),
        "s5_a_re": -0.5 * jnp.exp(nrm(ks[18], (N_S5_LAYERS, 2, G, P), 0.05)),
        "s5_a_im": a_im,
        "s5_log_dt": jax.random.uniform(ks[19], (N_S5_LAYERS, 2, G), f32,
                                        math.log(1e-3), math.log(1e-1)),
        "s5_b_re": nrm(ks[20], (N_S5_LAYERS, 2, G, P, Q), (2 * Q) ** -0.5),
        "s5_b_im": nrm(ks[21], (N_S5_LAYERS, 2, G, P, Q), (2 * Q) ** -0.5),
        "s5_c_re": nrm(ks[22], (N_S5_LAYERS, 2, G, Q, P), P ** -0.5),
        "s5_c_im": nrm(ks[23], (N_S5_LAYERS, 2, G, Q, P), P ** -0.5),
        "s5_d": nrm(ks[24], (N_S5_LAYERS, EB), 1.0),
        "s5_w_glu": nrm(ks[25], (N_S5_LAYERS, EB, 2 * D), EB ** -0.5),
        "final_g": 1.0 + nrm(ks[26], (D,), 0.02),
    }


def reference(x, c, ctx, c_ctx, ada_w, ada_b, norm1_g, norm2_g, ffn_w1, ffn_w3, ffn_w2,
              gm_w_in, gm_ln_g, gm_ln_b, gm_w_s, gm_b_s, gm_w_out,
              s5_w_in, s5_a_re, s5_a_im, s5_log_dt, s5_b_re, s5_b_im, s5_c_re, s5_c_im, s5_d, s5_w_glu,
              final_g):
    L = x.shape[1]
    h = x + sincos_2d(L).astype(x.dtype)[None]
    hc = ctx
    s5_layers = [i for i in range(DEPTH) if i % N_MIXERS == 1]
    last_s5 = s5_layers[-1] if s5_layers else -1
    cond_l = c[:, None, :]
    cond_c = c_ctx[None, None, :]
    for i in range(DEPTH):
        ctx_read = i <= last_s5
        ctx_carry = i < last_s5
        j = i // N_MIXERS
        sh1, sc1, g1, sh2, sc2, g2 = adaln(cond_l, ada_w[i], ada_b[i])
        hin = modulate(rmsnorm(h, norm1_g[i]), sh1, sc1)
        if ctx_read:
            csh1, csc1, cg1, csh2, csc2, cg2 = adaln(cond_c, ada_w[i], ada_b[i])
            cin = modulate(rmsnorm(hc, norm1_g[i]), csh1, csc1)
        if i % N_MIXERS == 0:
            gm = (gm_w_in[j], gm_ln_g[j], gm_ln_b[j], gm_w_s[j], gm_b_s[j], gm_w_out[j])
            h = h + g1 * gmlp_mixer(hin, *gm)
            if ctx_carry:
                hc = hc + cg1 * gmlp_mixer(cin, *gm)
        else:
            out_l, out_c = s5_mixer(hin, cin, s5_w_in[j], s5_a_re[j], s5_a_im[j], s5_log_dt[j],
                                    s5_b_re[j], s5_b_im[j], s5_c_re[j], s5_c_im[j], s5_d[j],
                                    s5_w_glu[j], ctx_carry)
            h = h + g1 * out_l
            if ctx_carry:
                hc = hc + cg1 * out_c
        h = h + g2 * swiglu(modulate(rmsnorm(h, norm2_g[i]), sh2, sc2), ffn_w1[i], ffn_w3[i], ffn_w2[i])
        if ctx_carry:
            hc = hc + cg2 * swiglu(modulate(rmsnorm(hc, norm2_g[i]), csh2, csc2),
                                   ffn_w1[i], ffn_w3[i], ffn_w2[i])
    return rmsnorm(h, final_g)
```

```python
import functools
import math

import jax
import jax.numpy as jnp
from jax import lax
from jax.experimental import pallas as pl
from jax.experimental.pallas import tpu as pltpu

F32 = jnp.float32
BF16 = jnp.bfloat16

EPS = 1e-6
GRID_W = 64
CHUNK = 128
GM_HEADS = 16
S5_GROUP = 16
S5_STATE = 64
N_MIXERS = 2

LANES = 128
S5_T = 8
S5_GB = LANES // S5_GROUP
S5_KW = S5_T * LANES
S5_SW = S5_GB * S5_STATE
VMEM_LIMIT = 56 * 1024 * 1024


def _cparams(sem):
    return pltpu.CompilerParams(dimension_semantics=sem, vmem_limit_bytes=VMEM_LIMIT)


def _adaln_kernel(cond_ref, w_ref, b_ref, o_ref, s_ref, *, n_rows, kc):
    x = cond_ref[...]
    s_ref[...] = x * jax.nn.sigmoid(x)
    d, tn = w_ref.shape
    nslab = tn // LANES

    def body(kk, accs):
        k0 = pl.multiple_of(kk * kc, kc)
        new = []
        for r in range(n_rows):
            sb = s_ref[r, pl.ds(k0, kc), :]
            for n in range(nslab):
                w = w_ref[pl.ds(k0, kc), n * LANES:(n + 1) * LANES]
                new.append(accs[r * nslab + n] + (w * sb).reshape(kc // 8, 8, LANES).sum(axis=0))
        return tuple(new)

    init = tuple(jnp.zeros((8, LANES), F32) for _ in range(n_rows * nslab))
    accs = lax.fori_loop(0, d // kc, body, init)
    o_ref[...] = jnp.zeros(o_ref.shape, F32)
    for r in range(n_rows):
        for n in range(nslab):
            o_ref[r:r + 1, n * LANES:(n + 1) * LANES] = (
                jnp.sum(accs[r * nslab + n], axis=0, keepdims=True) + b_ref[:, n * LANES:(n + 1) * LANES])


def _adaln_all(cond_rows, ada_w, ada_b):
    n_rows, d = cond_rows.shape
    depth, _, n6 = ada_w.shape
    tn = 1024
    cond_b = jnp.broadcast_to(cond_rows[:, :, None], (n_rows, d, LANES))
    return pl.pallas_call(
        functools.partial(_adaln_kernel, n_rows=n_rows, kc=64),
        out_shape=jax.ShapeDtypeStruct((depth, 8, n6), F32),
        grid=(depth, n6 // tn),
        in_specs=[
            pl.BlockSpec((n_rows, d, LANES), lambda l, j: (0, 0, 0)),
            pl.BlockSpec((None, d, tn), lambda l, j: (l, 0, j)),
            pl.BlockSpec((None, 1, tn), lambda l, j: (l, 0, j)),
        ],
        out_specs=pl.BlockSpec((None, 8, tn), lambda l, j: (l, 0, j)),
        scratch_shapes=[pltpu.VMEM((n_rows, d, LANES), F32)],
        compiler_params=_cparams(("arbitrary", "arbitrary")),
        name="adaln",
    )(cond_b, ada_w, ada_b.reshape(depth, 1, n6))


def _embed_kernel(x_ref, o_ref, col_ref, *, rows_per_tile):
    d = x_ref.shape[-1]
    q = d // 4
    k = lax.broadcasted_iota(jnp.int32, (1, q), 1).astype(F32)
    omega = jnp.exp(k * (-math.log(10000.0) / q))
    i = pl.program_id(0)

    @pl.when((i == 0) & (pl.program_id(1) == 0))
    def _():
        c = lax.broadcasted_iota(jnp.int32, (GRID_W, 1), 0).astype(F32)
        ang = c * omega
        col_ref[:, :q] = jnp.sin(ang)
        col_ref[:, q:] = jnp.cos(ang)

    r = (i * rows_per_tile + lax.broadcasted_iota(jnp.int32, (rows_per_tile, 1), 0)).astype(F32)
    ang_r = r * omega
    row_pe = jnp.concatenate([jnp.sin(ang_r), jnp.cos(ang_r)], axis=-1)
    for rr in range(rows_per_tile):
        sl = slice(rr * GRID_W, (rr + 1) * GRID_W)
        o_ref[sl, :2 * q] = x_ref[sl, :2 * q] + row_pe[rr:rr + 1, :]
        o_ref[sl, 2 * q:] = x_ref[sl, 2 * q:] + col_ref[...]


def _embed(x):
    b, l, d = x.shape
    rows_per_tile = 8
    tl = rows_per_tile * GRID_W
    return pl.pallas_call(
        functools.partial(_embed_kernel, rows_per_tile=rows_per_tile),
        out_shape=jax.ShapeDtypeStruct(x.shape, F32),
        grid=(l // tl, b),
        in_specs=[pl.BlockSpec((None, tl, d), lambda i, bb: (bb, i, 0))],
        out_specs=pl.BlockSpec((None, tl, d), lambda i, bb: (bb, i, 0)),
        scratch_shapes=[pltpu.VMEM((GRID_W, d // 2), F32)],
        compiler_params=_cparams(("arbitrary", "arbitrary")),
        name="embed",
    )(x)


def _rmsnorm_kernel(h_ref, g_ref, o_ref):
    x = h_ref[...]
    o_ref[...] = x * lax.rsqrt(jnp.mean(x * x, axis=-1, keepdims=True) + EPS) * g_ref[...]


def _final_norm(h, g):
    b, l, d = h.shape
    tm = 512
    return pl.pallas_call(
        _rmsnorm_kernel,
        out_shape=jax.ShapeDtypeStruct(h.shape, F32),
        grid=(b, l // tm),
        in_specs=[pl.BlockSpec((None, tm, d), lambda bb, i: (bb, i, 0)),
                  pl.BlockSpec((1, d), lambda bb, i: (0, 0))],
        out_specs=pl.BlockSpec((None, tm, d), lambda bb, i: (bb, i, 0)),
        compiler_params=_cparams(("arbitrary", "arbitrary")),
        name="final_norm",
    )(h, g.reshape(1, d))


def _normmod_matmul_kernel(h_ref, g_ref, sh_ref, sc_ref, *rest, n_w, mode):
    w_refs = rest[:n_w]
    out_refs = rest[n_w:-1]
    hin_ref = rest[-1]

    @pl.when(pl.program_id(2) == 0)
    def _():
        x = h_ref[...]
        y = x * lax.rsqrt(jnp.mean(x * x, axis=-1, keepdims=True) + EPS) * g_ref[...]
        hin_ref[...] = (y * (1.0 + sc_ref[...]) + sh_ref[...]).astype(BF16)

    a = hin_ref[...]
    accs = [jnp.dot(a, w[...], preferred_element_type=F32) for w in w_refs]
    if mode == "plain":
        out_refs[0][...] = accs[0].astype(out_refs[0].dtype)
    elif mode == "gelu_pair":
        out_refs[0][...] = jax.nn.gelu(accs[0]).astype(out_refs[0].dtype)
        out_refs[1][...] = jax.nn.gelu(accs[1]).astype(out_refs[1].dtype)
    elif mode == "swiglu":
        out_refs[0][...] = (jax.nn.silu(accs[0]) * accs[1]).astype(out_refs[0].dtype)
    else:
        raise ValueError(mode)


def _normmod_matmul(h, g, shift, scale, weights, mode, out_dtypes, tn, name):
    b, t, d = h.shape
    tm = min(1024, t)
    n = weights[0][2]
    n_w = len(weights)
    in_specs = [
        pl.BlockSpec((None, tm, d), lambda bb, i, j: (bb, i, 0)),
        pl.BlockSpec((1, d), lambda bb, i, j: (0, 0)),
        pl.BlockSpec((None, 1, d), lambda bb, i, j: (bb, 0, 0)),
        pl.BlockSpec((None, 1, d), lambda bb, i, j: (bb, 0, 0)),
    ]
    for _, off, _ in weights:
        in_specs.append(pl.BlockSpec((d, tn), functools.partial(lambda bb, i, j, o: (0, j + o), o=off // tn)))
    out_shape = [jax.ShapeDtypeStruct((b, t, n), dt) for dt in out_dtypes]
    out_specs = [pl.BlockSpec((None, tm, tn), lambda bb, i, j: (bb, i, j)) for _ in out_dtypes]
    outs = pl.pallas_call(
        functools.partial(_normmod_matmul_kernel, n_w=n_w, mode=mode),
        out_shape=out_shape,
        grid=(b, t // tm, n // tn),
        in_specs=in_specs,
        out_specs=out_specs,
        scratch_shapes=[pltpu.VMEM((tm, d), BF16)],
        compiler_params=_cparams(("arbitrary", "arbitrary", "arbitrary")),
        name=name,
    )(h, g.reshape(1, d), shift, scale, *[w for w, _, _ in weights])
    return outs


def _matmul_res_kernel(a_ref, h_ref, gate_ref, *rest, glu):
    a = a_ref[...]
    acc = jnp.dot(a, rest[0][...], preferred_element_type=F32)
    if glu:
        gl = jnp.dot(a, rest[1][...], preferred_element_type=F32)
        acc = acc * jax.nn.sigmoid(gl)
    o_ref = rest[-1]
    o_ref[...] = h_ref[...] + gate_ref[...] * acc


def _matmul_res(a, w, h, gate, glu, name):
    b, t, k = a.shape
    n = h.shape[-1]
    tm = min(512, t)
    tn = 512
    in_specs = [
        pl.BlockSpec((None, tm, k), lambda bb, i, j: (bb, i, 0)),
        pl.BlockSpec((None, tm, tn), lambda bb, i, j: (bb, i, j)),
        pl.BlockSpec((None, 1, tn), lambda bb, i, j: (bb, 0, j)),
        pl.BlockSpec((k, tn), lambda bb, i, j: (0, j)),
    ]
    args = [a, h, gate, w]
    if glu:
        in_specs.append(pl.BlockSpec((k, tn), functools.partial(lambda bb, i, j, o: (0, j + o), o=n // tn)))
        args.append(w)
    return pl.pallas_call(
        functools.partial(_matmul_res_kernel, glu=glu),
        out_shape=jax.ShapeDtypeStruct(h.shape, F32),
        grid=(b, t // tm, n // tn),
        in_specs=in_specs,
        out_specs=pl.BlockSpec((None, tm, tn), lambda bb, i, j: (bb, i, j)),
        compiler_params=_cparams(("arbitrary", "arbitrary", "arbitrary")),
        name=name,
    )(*args)


def _gmlp_gate_kernel(u_ref, v_ref, lng_ref, lnb_ref, ws_ref, bs_ref, t_ref):
    v = v_ref[...].astype(F32)
    mu = jnp.mean(v, axis=-1, keepdims=True)
    xc = v - mu
    var = jnp.mean(xc * xc, axis=-1, keepdims=True)
    vn = (xc * lax.rsqrt(var + EPS) * lng_ref[...] + lnb_ref[...]).astype(BF16)
    hd = v.shape[-1] // GM_HEADS
    for hh in range(GM_HEADS):
        sl = slice(hh * hd, (hh + 1) * hd)
        s = jnp.dot(ws_ref[hh], vn[:, sl], preferred_element_type=F32) + bs_ref[hh]
        t_ref[:, sl] = (u_ref[:, sl].astype(F32) * s).astype(BF16)


def _gmlp_gate(u, v, ln_g, ln_b, w_s, b_s):
    b, t, e = u.shape
    return pl.pallas_call(
        _gmlp_gate_kernel,
        out_shape=jax.ShapeDtypeStruct((b, t, e), BF16),
        grid=(b, t // CHUNK),
        in_specs=[
            pl.BlockSpec((None, CHUNK, e), lambda bb, i: (bb, i, 0)),
            pl.BlockSpec((None, CHUNK, e), lambda bb, i: (bb, i, 0)),
            pl.BlockSpec((1, e), lambda bb, i: (0, 0)),
            pl.BlockSpec((1, e), lambda bb, i: (0, 0)),
            pl.BlockSpec((GM_HEADS, CHUNK, CHUNK), lambda bb, i: (0, 0, 0)),
            pl.BlockSpec((GM_HEADS, CHUNK, 1), lambda bb, i: (0, 0, 0)),
        ],
        out_specs=pl.BlockSpec((None, CHUNK, e), lambda bb, i: (bb, i, 0)),
        compiler_params=_cparams(("arbitrary", "arbitrary")),
        name="gmlp_gate",
    )(u, v, ln_g.reshape(1, e), ln_b.reshape(1, e), w_s.astype(BF16), b_s.reshape(GM_HEADS, CHUNK, 1))


def _s5_matrices(a_re, a_im, log_dt, b_re, b_im, c_re, c_im):
    t = S5_T
    n_g = a_re.shape[1]
    n_blk = n_g // S5_GB
    hp = lax.Precision.HIGHEST
    dt = jnp.exp(log_dt)[..., None]
    lam_re, lam_im = a_re * dt, a_im * dt
    kk = jnp.arange(t + 1, dtype=F32)[:, None, None, None]
    mag = jnp.exp(kk * lam_re)
    pw_re, pw_im = mag * jnp.cos(kk * lam_im), mag * jnp.sin(kk * lam_im)
    nr, ni = pw_re[1] - 1.0, pw_im[1]
    den = a_re * a_re + a_im * a_im
    cf_re, cf_im = (nr * a_re + ni * a_im) / den, (ni * a_re - nr * a_im) / den
    bb_re = cf_re[..., None] * b_re - cf_im[..., None] * b_im
    bb_im = cf_re[..., None] * b_im + cf_im[..., None] * b_re

    pos = jnp.arange(t)
    e_in = jnp.stack([t - 1 - pos, pos])
    e_out = jnp.stack([pos + 1, t - pos])
    d_idx = jnp.arange(2)[:, None]
    pin_re, pin_im = pw_re[e_in, d_idx], pw_im[e_in, d_idx]
    pout_re, pout_im = pw_re[e_out, d_idx], pw_im[e_out, d_idx]

    min_re = jnp.einsum('dsgp,dgpq->dgsqp', pin_re, bb_re) - jnp.einsum('dsgp,dgpq->dgsqp', pin_im, bb_im)
    min_im = jnp.einsum('dsgp,dgpq->dgsqp', pin_re, bb_im) + jnp.einsum('dsgp,dgpq->dgsqp', pin_im, bb_re)
    m_in = jnp.stack([min_re, min_im], axis=4)

    ab_re = pw_re[:t, :, :, :, None] * bb_re[None] - pw_im[:t, :, :, :, None] * bb_im[None]
    ab_im = pw_re[:t, :, :, :, None] * bb_im[None] + pw_im[:t, :, :, :, None] * bb_re[None]
    lagk = (jnp.einsum('dgqp,kdgpr->kdgqr', c_re, ab_re, precision=hp)
            - jnp.einsum('dgqp,kdgpr->kdgqr', c_im, ab_im, precision=hp))
    s_i, t_i = pos[:, None], pos[None, :]
    lag_f = jnp.where(s_i <= t_i, t_i - s_i, 0)
    lag_b = jnp.where(s_i >= t_i, s_i - t_i, 0)
    kf = jnp.where((s_i <= t_i)[:, :, None, None, None], lagk[lag_f, 0], 0.0)
    kb = jnp.where((s_i >= t_i)[:, :, None, None, None], lagk[lag_b, 1], 0.0)
    m_intra = jnp.transpose(kf + kb, (2, 0, 4, 1, 3))

    mo_re = (jnp.einsum('dgqp,dtgp->dgptq', c_re, pout_re) - jnp.einsum('dgqp,dtgp->dgptq', c_im, pout_im))
    mo_im = -(jnp.einsum('dgqp,dtgp->dgptq', c_re, pout_im) + jnp.einsum('dgqp,dtgp->dgptq', c_im, pout_re))
    m_out = jnp.stack([mo_re, mo_im], axis=2)

    eye = jnp.eye(S5_GB, dtype=F32)
    q, p = S5_GROUP, S5_STATE
    m_in_bd = jnp.einsum('djgsqrp,gh->jdsgqrhp', m_in.reshape(2, n_blk, S5_GB, t, q, 2, p), eye)
    m_in_bd = m_in_bd.reshape(n_blk, 2, S5_KW, 2 * S5_SW).astype(BF16)
    m_intra_bd = jnp.einsum('jgsqtr,gh->jsgqthr', m_intra.reshape(n_blk, S5_GB, t, q, t, q), eye)
    m_intra_bd = m_intra_bd.reshape(n_blk, S5_KW, S5_KW).astype(BF16)
    m_out_bd = jnp.einsum('djgrptq,gh->jdrgpthq', m_out.reshape(2, n_blk, S5_GB, 2, p, t, q), eye)
    m_out_bd = m_out_bd.reshape(n_blk, 2, 2 * S5_SW, S5_KW).astype(BF16)
    decay = jnp.stack([pw_re[t], pw_im[t]], axis=1)
    decay = jnp.transpose(decay.reshape(2, 2, n_blk, S5_SW), (2, 0, 1, 3))
    return m_in_bd, m_intra_bd, m_out_bd, decay


def _s5_kernel(uc_ref, ul_ref, d_ref, min_ref, mintra_ref, mout_ref, dec_ref, zc_ref, zl_ref,
               ucb_ref, xh_ref, y_ref, *, nc_ctx, nc_lat, row_blk):
    nch = nc_ctx + nc_lat
    t = S5_T
    for tt in range(t):
        ucb_ref[0:nc_ctx, tt * LANES:(tt + 1) * LANES] = uc_ref[pl.ds(tt, nc_ctx, stride=t), :].astype(BF16)
        ucb_ref[nc_ctx:nch, tt * LANES:(tt + 1) * LANES] = ul_ref[pl.ds(tt, nc_lat, stride=t), :].astype(BF16)

    n_rb = nch // row_blk
    for dd in range(2):
        for rb in range(n_rb):
            rs = slice(rb * row_blk, (rb + 1) * row_blk)
            xh_ref[dd, rs, :] = jnp.dot(ucb_ref[rs, :], min_ref[dd], preferred_element_type=F32)

    af_re, af_im = dec_ref[0, 0:1, :], dec_ref[0, 1:2, :]
    ab_re, ab_im = dec_ref[1, 0:1, :], dec_ref[1, 1:2, :]
    sw = S5_SW

    def body(i, carry):
        hfr, hfi, hbr, hbi = carry
        cb = jnp.where(i < nc_ctx, nc_ctx - 1 - i, nch + nc_ctx - 1 - i)
        xfr = xh_ref[0, pl.ds(i, 1), 0:sw]
        xfi = xh_ref[0, pl.ds(i, 1), sw:2 * sw]
        xbr = xh_ref[1, pl.ds(cb, 1), 0:sw]
        xbi = xh_ref[1, pl.ds(cb, 1), sw:2 * sw]
        xh_ref[0, pl.ds(i, 1), 0:sw] = hfr
        xh_ref[0, pl.ds(i, 1), sw:2 * sw] = hfi
        xh_ref[1, pl.ds(cb, 1), 0:sw] = hbr
        xh_ref[1, pl.ds(cb, 1), sw:2 * sw] = hbi
        return (af_re * hfr - af_im * hfi + xfr, af_re * hfi + af_im * hfr + xfi,
                ab_re * hbr - ab_im * hbi + xbr, ab_re * hbi + ab_im * hbr + xbi)

    z0 = jnp.zeros((1, sw), F32)
    lax.fori_loop(0, nch, body, (z0, z0, z0, z0), unroll=8)

    for rb in range(n_rb):
        rs = slice(rb * row_blk, (rb + 1) * row_blk)
        y = jnp.dot(ucb_ref[rs, :], mintra_ref[...], preferred_element_type=F32)
        y = y + jnp.dot(xh_ref[0, rs, :].astype(BF16), mout_ref[0], preferred_element_type=F32)
        y = y + jnp.dot(xh_ref[1, rs, :].astype(BF16), mout_ref[1], preferred_element_type=F32)
        for tt in range(t):
            y_ref[pl.ds(rb * row_blk * t + tt, row_blk, stride=t), :] = y[:, tt * LANES:(tt + 1) * LANES]

    ctx_len = nc_ctx * t
    dv = d_ref[...]
    zc_ref[...] = jax.nn.gelu(y_ref[0:ctx_len, :] + uc_ref[...] * dv).astype(zc_ref.dtype)
    zl_ref[...] = jax.nn.gelu(y_ref[ctx_len:, :] + ul_ref[...] * dv).astype(zl_ref.dtype)


def _s5_core(u_ctx, u_lat, d_skip, mats):
    m_in, m_intra, m_out, decay = mats
    b, l, w = u_lat.shape
    ctx_len = u_ctx.shape[1]
    n_blk = w // LANES
    nc_ctx, nc_lat = ctx_len // S5_T, l // S5_T
    nch = nc_ctx + nc_lat
    row_blk = nch // 4
    assert row_blk % 8 == 0 and row_blk * 4 == nch and nc_ctx % 16 == 0
    kern = functools.partial(_s5_kernel, nc_ctx=nc_ctx, nc_lat=nc_lat, row_blk=row_blk)
    return pl.pallas_call(
        kern,
        out_shape=[jax.ShapeDtypeStruct(u_ctx.shape, BF16), jax.ShapeDtypeStruct(u_lat.shape, BF16)],
        grid=(n_blk, b),
        in_specs=[
            pl.BlockSpec((None, ctx_len, LANES), lambda j, bb: (bb, 0, j)),
            pl.BlockSpec((None, l, LANES), lambda j, bb: (bb, 0, j)),
            pl.BlockSpec((1, LANES), lambda j, bb: (0, j)),
            pl.BlockSpec((None, 2, S5_KW, 2 * S5_SW), lambda j, bb: (j, 0, 0, 0)),
            pl.BlockSpec((None, S5_KW, S5_KW), lambda j, bb: (j, 0, 0)),
            pl.BlockSpec((None, 2, 2 * S5_SW, S5_KW), lambda j, bb: (j, 0, 0, 0)),
            pl.BlockSpec((None, 2, 2, S5_SW), lambda j, bb: (j, 0, 0, 0)),
        ],
        out_specs=[
            pl.BlockSpec((None, ctx_len, LANES), lambda j, bb: (bb, 0, j)),
            pl.BlockSpec((None, l, LANES), lambda j, bb: (bb, 0, j)),
        ],
        scratch_shapes=[
            pltpu.VMEM((nch, S5_KW), BF16),
            pltpu.VMEM((2, nch, 2 * S5_SW), F32),
            pltpu.VMEM((nch * S5_T, LANES), F32),
        ],
        compiler_params=_cparams(("arbitrary", "arbitrary")),
        name="s5_core",
    )(u_ctx, u_lat, d_skip.reshape(1, w), m_in, m_intra, m_out, decay)


def kernel(x, c, ctx, c_ctx, ada_w, ada_b, norm1_g, norm2_g, ffn_w1, ffn_w3, ffn_w2, gm_w_in, gm_ln_g, gm_ln_b, gm_w_s, gm_b_s, gm_w_out, s5_w_in, s5_a_re, s5_a_im, s5_log_dt, s5_b_re, s5_b_im, s5_c_re, s5_c_im, s5_d, s5_w_glu, final_g):
    bsz, seq, d = x.shape
    depth = ada_w.shape[0]
    ffn_hidden = ffn_w1.shape[-1]
    gm_width = gm_w_out.shape[1]
    s5_layers = [i for i in range(depth) if i % N_MIXERS == 1]
    last_s5 = s5_layers[-1] if s5_layers else -1

    cond_rows = jnp.concatenate([c, c_ctx[None, :]], axis=0)
    mod = _adaln_all(cond_rows, ada_w, ada_b)

    def mods(i, stream):
        if stream == "lat":
            m = mod[i, :bsz]
        else:
            m = jnp.broadcast_to(mod[i, bsz:bsz + 1], (bsz, 6 * d))
        return [m[:, None, k * d:(k + 1) * d] for k in range(6)]

    h = _embed(x)
    hc = ctx

    for i in range(depth):
        ctx_read = i <= last_s5
        ctx_carry = i < last_s5
        j = i // N_MIXERS
        streams = [("lat", h)]
        if ctx_read:
            streams.append(("ctx", hc))
        new = {}
        if i % N_MIXERS == 0:
            w_in = gm_w_in[j].astype(BF16)
            w_out = gm_w_out[j].astype(BF16)
            for name, hs in streams:
                if name == "ctx" and not ctx_carry:
                    continue
                sh1, sc1, g1, _, _, _ = mods(i, name)
                u, v = _normmod_matmul(hs, norm1_g[i], sh1, sc1,
                                       [(w_in, 0, gm_width), (w_in, gm_width, gm_width)],
                                       "gelu_pair", [BF16, F32], 512, "gm_in")
                tg = _gmlp_gate(u, v, gm_ln_g[j], gm_ln_b[j], gm_w_s[j], gm_b_s[j])
                new[name] = _matmul_res(tg, w_out, hs, g1, False, "gm_out")
        else:
            w_in = s5_w_in[j].astype(BF16)
            w_glu = s5_w_glu[j].astype(BF16)
            mats = _s5_matrices(s5_a_re[j], s5_a_im[j], s5_log_dt[j], s5_b_re[j], s5_b_im[j],
                                s5_c_re[j], s5_c_im[j])
            us = {}
            for name, hs in streams:
                sh1, sc1, _, _, _, _ = mods(i, name)
                us[name] = _normmod_matmul(hs, norm1_g[i], sh1, sc1, [(w_in, 0, d)],
                                           "plain", [F32], 512, "s5_in")[0]
            z_ctx, z_lat = _s5_core(us["ctx"], us["lat"], s5_d[j], mats)
            new["lat"] = _matmul_res(z_lat, w_glu, h, mods(i, "lat")[2], True, "s5_out")
            if ctx_carry:
                new["ctx"] = _matmul_res(z_ctx, w_glu, hc, mods(i, "ctx")[2], True, "s5_out")
        h = new["lat"]
        if ctx_carry:
            hc = new["ctx"]

        w1 = ffn_w1[i].astype(BF16)
        w3 = ffn_w3[i].astype(BF16)
        w2 = ffn_w2[i].astype(BF16)
        streams = [("lat", h)] + ([("ctx", hc)] if ctx_carry else [])
        for name, hs in streams:
            _, _, _, sh2, sc2, g2 = mods(i, name)
            tf = _normmod_matmul(hs, norm2_g[i], sh2, sc2, [(w1, 0, ffn_hidden), (w3, 0, ffn_hidden)],
                                 "swiglu", [BF16], 512, "ffn_in")[0]
            out = _matmul_res(tf, w2, hs, g2, False, "ffn_out")
            if name == "lat":
                h = out
            else:
                hc = out

    return _final_norm(h, final_g)
```

```python
import functools
import math

import jax
import jax.numpy as jnp
from jax import lax
from jax.experimental import pallas as pl
from jax.experimental.pallas import tpu as pltpu

F32 = jnp.float32
BF16 = jnp.bfloat16

EPS = 1e-6
GRID_W = 64
CHUNK = 128
GM_HEADS = 16
S5_GROUP = 16
S5_STATE = 64
N_MIXERS = 2

LANES = 128
S5_T = 8
S5_GB = LANES // S5_GROUP
S5_KW = S5_T * LANES
S5_SW = S5_GB * S5_STATE
VMEM_LIMIT = 56 * 1024 * 1024


def _cparams(sem):
    return pltpu.CompilerParams(dimension_semantics=sem, vmem_limit_bytes=VMEM_LIMIT)


def _adaln_kernel(cond_ref, w_ref, b_ref, o_ref, s_ref, *, n_rows, kc):
    x = cond_ref[...]
    s_ref[...] = x * jax.nn.sigmoid(x)
    d, tn = w_ref.shape
    nslab = tn // LANES

    def body(kk, accs):
        k0 = pl.multiple_of(kk * kc, kc)
        new = []
        for r in range(n_rows):
            sb = s_ref[r, pl.ds(k0, kc), :]
            for n in range(nslab):
                w = w_ref[pl.ds(k0, kc), n * LANES:(n + 1) * LANES]
                new.append(accs[r * nslab + n] + (w * sb).reshape(kc // 8, 8, LANES).sum(axis=0))
        return tuple(new)

    init = tuple(jnp.zeros((8, LANES), F32) for _ in range(n_rows * nslab))
    accs = lax.fori_loop(0, d // kc, body, init)
    o_ref[...] = jnp.zeros(o_ref.shape, F32)
    for r in range(n_rows):
        for n in range(nslab):
            o_ref[r:r + 1, n * LANES:(n + 1) * LANES] = (
                jnp.sum(accs[r * nslab + n], axis=0, keepdims=True) + b_ref[:, n * LANES:(n + 1) * LANES])


def _adaln_all(cond_rows, ada_w, ada_b):
    n_rows, d = cond_rows.shape
    depth, _, n6 = ada_w.shape
    tn = 1024
    cond_b = jnp.broadcast_to(cond_rows[:, :, None], (n_rows, d, LANES))
    return pl.pallas_call(
        functools.partial(_adaln_kernel, n_rows=n_rows, kc=64),
        out_shape=jax.ShapeDtypeStruct((depth, 8, n6), F32),
        grid=(depth, n6 // tn),
        in_specs=[
            pl.BlockSpec((n_rows, d, LANES), lambda l, j: (0, 0, 0)),
            pl.BlockSpec((None, d, tn), lambda l, j: (l, 0, j)),
            pl.BlockSpec((None, 1, tn), lambda l, j: (l, 0, j)),
        ],
        out_specs=pl.BlockSpec((None, 8, tn), lambda l, j: (l, 0, j)),
        scratch_shapes=[pltpu.VMEM((n_rows, d, LANES), F32)],
        compiler_params=_cparams(("arbitrary", "arbitrary")),
        name="adaln",
    )(cond_b, ada_w, ada_b.reshape(depth, 1, n6))


def _embed_kernel(x_ref, o_ref, col_ref, *, rows_per_tile):
    d = x_ref.shape[-1]
    q = d // 4
    k = lax.broadcasted_iota(jnp.int32, (1, q), 1).astype(F32)
    omega = jnp.exp(k * (-math.log(10000.0) / q))
    i = pl.program_id(0)

    @pl.when((i == 0) & (pl.program_id(1) == 0))
    def _():
        c = lax.broadcasted_iota(jnp.int32, (GRID_W, 1), 0).astype(F32)
        ang = c * omega
        col_ref[:, :q] = jnp.sin(ang)
        col_ref[:, q:] = jnp.cos(ang)

    r = (i * rows_per_tile + lax.broadcasted_iota(jnp.int32, (rows_per_tile, 1), 0)).astype(F32)
    ang_r = r * omega
    row_pe = jnp.concatenate([jnp.sin(ang_r), jnp.cos(ang_r)], axis=-1)
    for rr in range(rows_per_tile):
        sl = slice(rr * GRID_W, (rr + 1) * GRID_W)
        o_ref[sl, :2 * q] = x_ref[sl, :2 * q] + row_pe[rr:rr + 1, :]
        o_ref[sl, 2 * q:] = x_ref[sl, 2 * q:] + col_ref[...]


def _embed(x):
    b, l, d = x.shape
    rows_per_tile = 8
    tl = rows_per_tile * GRID_W
    return pl.pallas_call(
        functools.partial(_embed_kernel, rows_per_tile=rows_per_tile),
        out_shape=jax.ShapeDtypeStruct(x.shape, F32),
        grid=(l // tl, b),
        in_specs=[pl.BlockSpec((None, tl, d), lambda i, bb: (bb, i, 0))],
        out_specs=pl.BlockSpec((None, tl, d), lambda i, bb: (bb, i, 0)),
        scratch_shapes=[pltpu.VMEM((GRID_W, d // 2), F32)],
        compiler_params=_cparams(("arbitrary", "arbitrary")),
        name="embed",
    )(x)


def _rmsnorm_kernel(h_ref, g_ref, o_ref):
    x = h_ref[...]
    o_ref[...] = x * lax.rsqrt(jnp.mean(x * x, axis=-1, keepdims=True) + EPS) * g_ref[...]


def _final_norm(h, g):
    b, l, d = h.shape
    tm = 512
    return pl.pallas_call(
        _rmsnorm_kernel,
        out_shape=jax.ShapeDtypeStruct(h.shape, F32),
        grid=(b, l // tm),
        in_specs=[pl.BlockSpec((None, tm, d), lambda bb, i: (bb, i, 0)),
                  pl.BlockSpec((1, d), lambda bb, i: (0, 0))],
        out_specs=pl.BlockSpec((None, tm, d), lambda bb, i: (bb, i, 0)),
        compiler_params=_cparams(("arbitrary", "arbitrary")),
        name="final_norm",
    )(h, g.reshape(1, d))


def _w_index(bb, i, j, *, layer, off):
    return (layer, 0, j + off)


def _normmod_matmul_kernel(h_ref, g_ref, sh_ref, sc_ref, *rest, n_w, mode):
    w_refs = rest[:n_w]
    out_refs = rest[n_w:-1]
    hin_ref = rest[-1]

    @pl.when(pl.program_id(2) == 0)
    def _():
        x = h_ref[...]
        r = lax.rsqrt(jnp.mean(x * x, axis=-1, keepdims=True) + EPS)
        gs = g_ref[...] * (1.0 + sc_ref[...])
        hin_ref[...] = (x * r * gs + sh_ref[...]).astype(BF16)

    a = hin_ref[...]
    accs = [jnp.dot(a, w[...], preferred_element_type=F32) for w in w_refs]
    if mode == "plain":
        out_refs[0][...] = accs[0].astype(out_refs[0].dtype)
    elif mode == "gelu_pair":
        out_refs[0][...] = jax.nn.gelu(accs[0]).astype(out_refs[0].dtype)
        out_refs[1][...] = jax.nn.gelu(accs[1]).astype(out_refs[1].dtype)
    elif mode == "swiglu":
        out_refs[0][...] = (jax.nn.silu(accs[0]) * accs[1]).astype(out_refs[0].dtype)
    else:
        raise ValueError(mode)


def _normmod_matmul(h, g, shift, scale, weights, n, mode, out_dtypes, tn, name):
    b, t, d = h.shape
    tm = min(1024, t)
    in_specs = [
        pl.BlockSpec((None, tm, d), lambda bb, i, j: (bb, i, 0)),
        pl.BlockSpec((1, d), lambda bb, i, j: (0, 0)),
        pl.BlockSpec((None, 1, d), lambda bb, i, j: (bb, 0, 0)),
        pl.BlockSpec((None, 1, d), lambda bb, i, j: (bb, 0, 0)),
    ]
    for _, layer, off in weights:
        in_specs.append(pl.BlockSpec((None, d, tn), functools.partial(_w_index, layer=layer, off=off // tn)))
    out_shape = [jax.ShapeDtypeStruct((b, t, n), dt) for dt in out_dtypes]
    out_specs = [pl.BlockSpec((None, tm, tn), lambda bb, i, j: (bb, i, j)) for _ in out_dtypes]
    return pl.pallas_call(
        functools.partial(_normmod_matmul_kernel, n_w=len(weights), mode=mode),
        out_shape=out_shape,
        grid=(b, t // tm, n // tn),
        in_specs=in_specs,
        out_specs=out_specs,
        scratch_shapes=[pltpu.VMEM((tm, d), BF16)],
        compiler_params=_cparams(("arbitrary", "arbitrary", "arbitrary")),
        name=name,
    )(h, g.reshape(1, d), shift, scale, *[w for w, _, _ in weights])


def _matmul_res_kernel(a_ref, h_ref, gate_ref, *rest, glu):
    a = a_ref[...]
    acc = jnp.dot(a, rest[0][...], preferred_element_type=F32)
    if glu:
        gl = jnp.dot(a, rest[1][...], preferred_element_type=F32)
        acc = acc * jax.nn.sigmoid(gl)
    o_ref = rest[-1]
    o_ref[...] = h_ref[...] + gate_ref[...] * acc


def _matmul_res(a, w, layer, h, gate, glu, name):
    b, t, k = a.shape
    n = h.shape[-1]
    tm = min(1024, t)
    tn = 512
    in_specs = [
        pl.BlockSpec((None, tm, k), lambda bb, i, j: (bb, i, 0)),
        pl.BlockSpec((None, tm, tn), lambda bb, i, j: (bb, i, j)),
        pl.BlockSpec((None, 1, tn), lambda bb, i, j: (bb, 0, j)),
        pl.BlockSpec((None, k, tn), functools.partial(_w_index, layer=layer, off=0)),
    ]
    args = [a, h, gate, w]
    if glu:
        in_specs.append(pl.BlockSpec((None, k, tn), functools.partial(_w_index, layer=layer, off=n // tn)))
        args.append(w)
    return pl.pallas_call(
        functools.partial(_matmul_res_kernel, glu=glu),
        out_shape=jax.ShapeDtypeStruct(h.shape, F32),
        grid=(b, t // tm, n // tn),
        in_specs=in_specs,
        out_specs=pl.BlockSpec((None, tm, tn), lambda bb, i, j: (bb, i, j)),
        compiler_params=_cparams(("arbitrary", "arbitrary", "arbitrary")),
        name=name,
    )(*args)


def _gmlp_gate_kernel(u_ref, v_ref, lng_ref, lnb_ref, ws_ref, bs_ref, t_ref):
    v = v_ref[...].astype(F32)
    mu = jnp.mean(v, axis=-1, keepdims=True)
    xc = v - mu
    var = jnp.mean(xc * xc, axis=-1, keepdims=True)
    vn = (xc * lax.rsqrt(var + EPS) * lng_ref[...] + lnb_ref[...]).astype(BF16)
    hd = v.shape[-1] // GM_HEADS
    for hh in range(GM_HEADS):
        sl = slice(hh * hd, (hh + 1) * hd)
        s = jnp.dot(ws_ref[hh], vn[:, sl], preferred_element_type=F32) + bs_ref[hh]
        t_ref[:, sl] = (u_ref[:, sl].astype(F32) * s).astype(BF16)


def _gmlp_gate(u, v, ln_g, ln_b, w_s, b_s):
    b, t, e = u.shape
    return pl.pallas_call(
        _gmlp_gate_kernel,
        out_shape=jax.ShapeDtypeStruct((b, t, e), BF16),
        grid=(b, t // CHUNK),
        in_specs=[
            pl.BlockSpec((None, CHUNK, e), lambda bb, i: (bb, i, 0)),
            pl.BlockSpec((None, CHUNK, e), lambda bb, i: (bb, i, 0)),
            pl.BlockSpec((1, e), lambda bb, i: (0, 0)),
            pl.BlockSpec((1, e), lambda bb, i: (0, 0)),
            pl.BlockSpec((GM_HEADS, CHUNK, CHUNK), lambda bb, i: (0, 0, 0)),
            pl.BlockSpec((GM_HEADS, CHUNK, 1), lambda bb, i: (0, 0, 0)),
        ],
        out_specs=pl.BlockSpec((None, CHUNK, e), lambda bb, i: (bb, i, 0)),
        compiler_params=_cparams(("arbitrary", "arbitrary")),
        name="gmlp_gate",
    )(u, v, ln_g.reshape(1, e), ln_b.reshape(1, e), w_s.astype(BF16), b_s.reshape(GM_HEADS, CHUNK, 1))


def _dot_split(a, b):
    a_hi, b_hi = a.astype(BF16), b.astype(BF16)
    a_lo = (a - a_hi.astype(F32)).astype(BF16)
    b_lo = (b - b_hi.astype(F32)).astype(BF16)
    dot = functools.partial(jnp.dot, preferred_element_type=F32)
    return dot(a_hi, b_hi) + (dot(a_hi, b_lo) + dot(a_lo, b_hi))


def _s5_ops_kernel(lane_ref, bt_ref, ct_ref, min_ref, mintra_ref, mout_ref, dec_ref):
    t = S5_T
    sw = S5_SW
    mask_b =(lax.broadcasted_iota(jnp.int32, (LANES, sw), 0) // S5_GROUP
              == lax.broadcasted_iota(jnp.int32, (LANES, sw), 1) // S5_STATE)
    mask_c = (lax.broadcasted_iota(jnp.int32, (sw, LANES), 0) // S5_STATE
              == lax.broadcasted_iota(jnp.int32, (sw, LANES), 1) // S5_GROUP)

    def to_rows(v):
        return jnp.concatenate(
            [jnp.broadcast_to(v[:, i * LANES:(i + 1) * LANES], (LANES, LANES)).T for i in range(sw // LANES)],
            axis=0)

    lag = []
    for dd in range(2):
        a_re, a_im = lane_ref[dd, 0:1, :], lane_ref[dd, 1:2, :]
        dt = jnp.exp(lane_ref[dd, 2:3, :])
        mag = jnp.exp(a_re * dt)
        p1_re, p1_im = mag * jnp.cos(a_im * dt), mag * jnp.sin(a_im * dt)
        nr, ni = p1_re - 1.0, p1_im
        den = a_re * a_re + a_im * a_im
        cf_re, cf_im = (nr * a_re + ni * a_im) / den, (ni * a_re - nr * a_im) / den
        b_re = jnp.where(mask_b, bt_ref[dd, 0], 0.0)
        b_im = jnp.where(mask_b, bt_ref[dd, 1], 0.0)
        bb_re = cf_re * b_re - cf_im * b_im
        bb_im = cf_re * b_im + cf_im * b_re
        pw = [(jnp.ones_like(p1_re), jnp.zeros_like(p1_re))]
        for _ in range(t):
            pr, pi = pw[-1]
            pw.append((pr * p1_re - pi * p1_im, pr * p1_im + pi * p1_re))
        dec_ref[dd, 0:1, :] = pw[t][0]
        dec_ref[dd, 1:2, :] = pw[t][1]
        ab = [(pr * bb_re - pi * bb_im, pr * bb_im + pi * bb_re) for pr, pi in pw[:t]]
        for s in range(t):
            m_re, m_im = ab[t - 1 - s] if dd == 0 else ab[s]
            min_ref[dd, s * LANES:(s + 1) * LANES, 0:sw] = m_re.astype(BF16)
            min_ref[dd, s * LANES:(s + 1) * LANES, sw:2 * sw] = m_im.astype(BF16)
        r1_re, r1_im = to_rows(p1_re), to_rows(p1_im)
        qr = jnp.where(mask_c, ct_ref[dd, 0], 0.0)
        qi = jnp.where(mask_c, ct_ref[dd, 1], 0.0)
        lag.append(_dot_split(jnp.concatenate([x[0] for x in ab], axis=0), qr)
                   - _dot_split(jnp.concatenate([x[1] for x in ab], axis=0), qi))
        mo = []
        for _ in range(t + 1):
            mo.append((qr, -qi))
            qr, qi = qr * r1_re - qi * r1_im, qr * r1_im + qi * r1_re
        for tt in range(t):
            m_re, m_im = mo[tt + 1] if dd == 0 else mo[t - tt]
            mout_ref[dd, 0:sw, tt * LANES:(tt + 1) * LANES] = m_re.astype(BF16)
            mout_ref[dd, sw:2 * sw, tt * LANES:(tt + 1) * LANES] = m_im.astype(BF16)

    def lag_blk(dd, k):
        return lag[dd][k * LANES:(k + 1) * LANES, :]

    for s in range(t):
        for tt in range(t):
            if s < tt:
                blk = lag_blk(0, tt - s)
            elif s > tt:
                blk = lag_blk(1, s - tt)
            else:
                blk = lag_blk(0, 0) + lag_blk(1, 0)
            mintra_ref[s * LANES:(s + 1) * LANES, tt * LANES:(tt + 1) * LANES] = blk.astype(BF16)


def _s5_ops(a_re, a_im, log_dt, b_re, b_im, c_re, c_im):
    nl, _, n_g, p = a_re.shape
    q = b_re.shape[-1]
    n_blk = n_g // S5_GB
    nb = nl * n_blk
    lane = jnp.stack([a_re, a_im, jnp.broadcast_to(log_dt[..., None], a_re.shape)], axis=2)
    lane = jnp.transpose(lane.reshape(nl, 2, 3, n_blk, S5_SW), (0, 3, 1, 2, 4)).reshape(nb, 2, 3, S5_SW)
    bt = jnp.stack([b_re, b_im], axis=2)
    bt = jnp.swapaxes(bt, -1, -2).reshape(nl, 2, 2, n_blk, S5_GB * q, p)
    bt = jnp.tile(jnp.transpose(bt, (0, 3, 1, 2, 4, 5)), (1, 1, 1, 1, 1, S5_GB)).reshape(nb, 2, 2, LANES, S5_SW)
    ct = jnp.stack([c_re, c_im], axis=2)
    ct = jnp.swapaxes(ct, -1, -2).reshape(nl, 2, 2, n_blk, S5_SW, q)
    ct = jnp.tile(jnp.transpose(ct, (0, 3, 1, 2, 4, 5)), (1, 1, 1, 1, 1, S5_GB)).reshape(nb, 2, 2, S5_SW, LANES)
    return pl.pallas_call(
        _s5_ops_kernel,
        out_shape=[
            jax.ShapeDtypeStruct((nb, 2, S5_KW, 2 * S5_SW), BF16),
            jax.ShapeDtypeStruct((nb, S5_KW, S5_KW), BF16),
            jax.ShapeDtypeStruct((nb, 2, 2 * S5_SW, S5_KW), BF16),
            jax.ShapeDtypeStruct((nb, 2, 2, S5_SW), F32),
        ],
        grid=(nb,),
        in_specs=[
            pl.BlockSpec((None, 2, 3, S5_SW), lambda j: (j, 0, 0, 0)),
            pl.BlockSpec((None, 2, 2, LANES, S5_SW), lambda j: (j, 0, 0, 0, 0)),
            pl.BlockSpec((None, 2, 2, S5_SW, LANES), lambda j: (j, 0, 0, 0, 0)),
        ],
        out_specs=[
            pl.BlockSpec((None, 2, S5_KW, 2 * S5_SW), lambda j: (j, 0, 0, 0)),
            pl.BlockSpec((None, S5_KW, S5_KW), lambda j: (j, 0, 0)),
            pl.BlockSpec((None, 2, 2 * S5_SW, S5_KW), lambda j: (j, 0, 0, 0)),
            pl.BlockSpec((None, 2, 2, S5_SW), lambda j: (j, 0, 0, 0)),
        ],
        compiler_params=_cparams(("arbitrary",)),
        name="s5_ops",
    )(lane, bt, ct)


def _s5_kernel(uc_ref, ul_ref, d_ref, min_ref, mintra_ref, mout_ref, dec_ref, zc_ref, zl_ref,
               ucb_ref, xh_ref, y_ref, *, nc_ctx, nc_lat, row_blk):
    nch = nc_ctx + nc_lat
    t = S5_T
    for tt in range(t):
        ucb_ref[0:nc_ctx, tt * LANES:(tt + 1) * LANES] = uc_ref[pl.ds(tt, nc_ctx, stride=t), :].astype(BF16)
        ucb_ref[nc_ctx:nch, tt * LANES:(tt + 1) * LANES] = ul_ref[pl.ds(tt, nc_lat, stride=t), :].astype(BF16)

    n_rb = nch // row_blk
    for dd in range(2):
        for rb in range(n_rb):
            rs = slice(rb * row_blk, (rb + 1) * row_blk)
            xh_ref[dd, rs, :] = jnp.dot(ucb_ref[rs, :], min_ref[dd], preferred_element_type=F32)

    af_re, af_im = dec_ref[0, 0:1, :], dec_ref[0, 1:2, :]
    ab_re, ab_im = dec_ref[1, 0:1, :], dec_ref[1, 1:2, :]
    sw = S5_SW

    def body(i, carry):
        hfr, hfi, hbr, hbi = carry
        cb = jnp.where(i < nc_ctx, nc_ctx - 1 - i, nch + nc_ctx - 1 - i)
        xfr = xh_ref[0, pl.ds(i, 1), 0:sw]
        xfi = xh_ref[0, pl.ds(i, 1), sw:2 * sw]
        xbr = xh_ref[1, pl.ds(cb, 1), 0:sw]
        xbi = xh_ref[1, pl.ds(cb, 1), sw:2 * sw]
        xh_ref[0, pl.ds(i, 1), 0:sw] = hfr
        xh_ref[0, pl.ds(i, 1), sw:2 * sw] = hfi
        xh_ref[1, pl.ds(cb, 1), 0:sw] = hbr
        xh_ref[1, pl.ds(cb, 1), sw:2 * sw] = hbi
        return (af_re * hfr - af_im * hfi + xfr, af_re * hfi + af_im * hfr + xfi,
                ab_re * hbr - ab_im * hbi + xbr, ab_re * hbi + ab_im * hbr + xbi)

    z0 = jnp.zeros((1, sw), F32)
    lax.fori_loop(0, nch, body, (z0, z0, z0, z0), unroll=8)

    for rb in range(n_rb):
        rs = slice(rb * row_blk, (rb + 1) * row_blk)
        y = jnp.dot(ucb_ref[rs, :], mintra_ref[...], preferred_element_type=F32)
        y = y + jnp.dot(xh_ref[0, rs, :].astype(BF16), mout_ref[0], preferred_element_type=F32)
        y = y + jnp.dot(xh_ref[1, rs, :].astype(BF16), mout_ref[1], preferred_element_type=F32)
        for tt in range(t):
            y_ref[pl.ds(rb * row_blk * t + tt, row_blk, stride=t), :] = y[:, tt * LANES:(tt + 1) * LANES]

    ctx_len = nc_ctx * t
    dv = d_ref[...]
    zc_ref[...] = jax.nn.gelu(y_ref[0:ctx_len, :] + uc_ref[...] * dv).astype(zc_ref.dtype)
    zl_ref[...] = jax.nn.gelu(y_ref[ctx_len:, :] + ul_ref[...] * dv).astype(zl_ref.dtype)


def _s5_core(u_ctx, u_lat, d_skip, ops, layer):
    m_in, m_intra, m_out, decay = ops
    b, l, w = u_lat.shape
    ctx_len = u_ctx.shape[1]
    n_blk = w // LANES
    nc_ctx, nc_lat = ctx_len // S5_T, l // S5_T
    nch = nc_ctx + nc_lat
    row_blk = nch // 4
    assert row_blk % 8 == 0 and row_blk * 4 == nch and nc_ctx % 16 == 0
    kern = functools.partial(_s5_kernel, nc_ctx=nc_ctx, nc_lat=nc_lat, row_blk=row_blk)
    base = layer * n_blk
    return pl.pallas_call(
        kern,
        out_shape=[jax.ShapeDtypeStruct(u_ctx.shape, BF16), jax.ShapeDtypeStruct(u_lat.shape, BF16)],
        grid=(n_blk, b),
        in_specs=[
            pl.BlockSpec((None, ctx_len, LANES), lambda j, bb: (bb, 0, j)),
            pl.BlockSpec((None, l, LANES), lambda j, bb: (bb, 0, j)),
            pl.BlockSpec((None, 1, LANES), lambda j, bb: (layer, 0, j)),
            pl.BlockSpec((None, 2, S5_KW, 2 * S5_SW), lambda j, bb: (base + j, 0, 0, 0)),
            pl.BlockSpec((None, S5_KW, S5_KW), lambda j, bb: (base + j, 0, 0)),
            pl.BlockSpec((None, 2, 2 * S5_SW, S5_KW), lambda j, bb: (base + j, 0, 0, 0)),
            pl.BlockSpec((None, 2, 2, S5_SW), lambda j, bb: (base + j, 0, 0, 0)),
        ],
        out_specs=[
            pl.BlockSpec((None, ctx_len, LANES), lambda j, bb: (bb, 0, j)),
            pl.BlockSpec((None, l, LANES), lambda j, bb: (bb, 0, j)),
        ],
        scratch_shapes=[
            pltpu.VMEM((nch, S5_KW), BF16),
            pltpu.VMEM((2, nch, 2 * S5_SW), F32),
            pltpu.VMEM((nch * S5_T, LANES), F32),
        ],
        compiler_params=_cparams(("arbitrary", "arbitrary")),
        name="s5_core",
    )(u_ctx, u_lat, d_skip.reshape(d_skip.shape[0], 1, w), m_in, m_intra, m_out, decay)


def kernel(x, c, ctx, c_ctx, ada_w, ada_b, norm1_g, norm2_g, ffn_w1, ffn_w3, ffn_w2, gm_w_in, gm_ln_g, gm_ln_b, gm_w_s, gm_b_s, gm_w_out, s5_w_in, s5_a_re, s5_a_im, s5_log_dt, s5_b_re, s5_b_im, s5_c_re, s5_c_im, s5_d, s5_w_glu, final_g):
    bsz, seq, d = x.shape
    depth = ada_w.shape[0]
    ffn_hidden = ffn_w1.shape[-1]
    gm_width = gm_w_out.shape[1]
    s5_layers = [i for i in range(depth) if i % N_MIXERS == 1]
    last_s5 = s5_layers[-1] if s5_layers else -1

    cond_rows = jnp.concatenate([c, c_ctx[None, :]], axis=0)
    mod = _adaln_all(cond_rows, ada_w, ada_b)

    def mods(i, stream):
        if stream == "lat":
            m = mod[i, :bsz]
        else:
            m = jnp.broadcast_to(mod[i, bsz:bsz + 1], (bsz, 6 * d))
        return [m[:, None, k * d:(k + 1) * d] for k in range(6)]

    w1, w3, w2 = ffn_w1.astype(BF16), ffn_w3.astype(BF16), ffn_w2.astype(BF16)
    gw_in, gw_out = gm_w_in.astype(BF16), gm_w_out.astype(BF16)
    sw_in, sw_glu = s5_w_in.astype(BF16), s5_w_glu.astype(BF16)
    s5_ops = _s5_ops(s5_a_re, s5_a_im, s5_log_dt, s5_b_re, s5_b_im, s5_c_re, s5_c_im) if s5_layers else None

    h = _embed(x)
    hc = ctx

    for i in range(depth):
        ctx_read = i <= last_s5
        ctx_carry = i < last_s5
        j = i // N_MIXERS
        streams = [("lat", h)]
        if ctx_read:
            streams.append(("ctx", hc))
        new = {}
        if i % N_MIXERS == 0:
            for name, hs in streams:
                if name == "ctx" and not ctx_carry:
                    continue
                sh1, sc1, g1, _, _, _ = mods(i, name)
                u, v = _normmod_matmul(hs, norm1_g[i], sh1, sc1, [(gw_in, j, 0), (gw_in, j, gm_width)],
                                       gm_width, "gelu_pair", [BF16, F32], 512, "gm_in")
                tg = _gmlp_gate(u, v, gm_ln_g[j], gm_ln_b[j], gm_w_s[j], gm_b_s[j])
                new[name] = _matmul_res(tg, gw_out, j, hs, g1, False, "gm_out")
        else:
            us = {}
            for name, hs in streams:
                sh1, sc1, _, _, _, _ = mods(i, name)
                us[name] = _normmod_matmul(hs, norm1_g[i], sh1, sc1, [(sw_in, j, 0)],
                                           d, "plain", [F32], 512, "s5_in")[0]
            z_ctx, z_lat = _s5_core(us["ctx"], us["lat"], s5_d, s5_ops, j)
            new["lat"] = _matmul_res(z_lat, sw_glu, j, h, mods(i, "lat")[2], True, "s5_out")
            if ctx_carry:
                new["ctx"] = _matmul_res(z_ctx, sw_glu, j, hc, mods(i, "ctx")[2], True, "s5_out")
        h = new["lat"]
        if ctx_carry:
            hc = new["ctx"]

        streams = [("lat", h)] + ([("ctx", hc)] if ctx_carry else [])
        for name, hs in streams:
            _, _, _, sh2, sc2, g2 = mods(i, name)
            tf = _normmod_matmul(hs, norm2_g[i], sh2, sc2, [(w1, i, 0), (w3, i, 0)],
                                 ffn_hidden, "swiglu", [BF16], 512, "ffn_in")[0]
            out = _matmul_res(tf, w2, i, hs, g2, False, "ffn_out")
            if name == "lat":
                h = out
            else:
                hc = out

    return _final_norm(h, final_g)
```

```python
import functools
import math

import jax
import jax.numpy as jnp
from jax import lax
from jax.experimental import pallas as pl
from jax.experimental.pallas import tpu as pltpu

F32 = jnp.float32
BF16 = jnp.bfloat16

EPS = 1e-6
GRID_W = 64
CHUNK = 128
GM_HEADS = 16
S5_GROUP = 16
S5_STATE = 64
N_MIXERS = 2

LANES = 128
S5_T = 8
S5_GB = LANES // S5_GROUP
S5_KW = S5_T * LANES
S5_SW = S5_GB * S5_STATE
VMEM_LIMIT = 56 * 1024 * 1024


def _cparams(sem):
    return pltpu.CompilerParams(dimension_semantics=sem, vmem_limit_bytes=VMEM_LIMIT)


def _adaln_kernel(cond_ref, w_ref, b_ref, o_ref, s_ref, *, n_rows, kc):
    x = cond_ref[...]
    s_ref[...] = x * jax.nn.sigmoid(x)
    d, tn = w_ref.shape
    nslab = tn // LANES

    def body(kk, accs):
        k0 = pl.multiple_of(kk * kc, kc)
        new = []
        for r in range(n_rows):
            sb = s_ref[r, pl.ds(k0, kc), :]
            for n in range(nslab):
                w = w_ref[pl.ds(k0, kc), n * LANES:(n + 1) * LANES]
                new.append(accs[r * nslab + n] + (w * sb).reshape(kc // 8, 8, LANES).sum(axis=0))
        return tuple(new)

    init = tuple(jnp.zeros((8, LANES), F32) for _ in range(n_rows * nslab))
    accs = lax.fori_loop(0, d // kc, body, init)
    o_ref[...] = jnp.zeros(o_ref.shape, F32)
    for r in range(n_rows):
        for n in range(nslab):
            o_ref[r:r + 1, n * LANES:(n + 1) * LANES] = (
                jnp.sum(accs[r * nslab + n], axis=0, keepdims=True) + b_ref[:, n * LANES:(n + 1) * LANES])


def _adaln_all(cond_rows, ada_w, ada_b):
    n_rows, d = cond_rows.shape
    depth, _, n6 = ada_w.shape
    tn = 1024
    cond_b = jnp.broadcast_to(cond_rows[:, :, None], (n_rows, d, LANES))
    return pl.pallas_call(
        functools.partial(_adaln_kernel, n_rows=n_rows, kc=64),
        out_shape=jax.ShapeDtypeStruct((depth, 8, n6), F32),
        grid=(depth, n6 // tn),
        in_specs=[
            pl.BlockSpec((n_rows, d, LANES), lambda l, j: (0, 0, 0)),
            pl.BlockSpec((None, d, tn), lambda l, j: (l, 0, j)),
            pl.BlockSpec((None, 1, tn), lambda l, j: (l, 0, j)),
        ],
        out_specs=pl.BlockSpec((None, 8, tn), lambda l, j: (l, 0, j)),
        scratch_shapes=[pltpu.VMEM((n_rows, d, LANES), F32)],
        compiler_params=_cparams(("arbitrary", "arbitrary")),
        name="adaln",
    )(cond_b, ada_w, ada_b.reshape(depth, 1, n6))


def _embed_kernel(x_ref, o_ref, col_ref, *, rows_per_tile):
    d = x_ref.shape[-1]
    q = d // 4
    k = lax.broadcasted_iota(jnp.int32, (1, q), 1).astype(F32)
    omega = jnp.exp(k * (-math.log(10000.0) / q))
    i = pl.program_id(0)

    @pl.when((i == 0) & (pl.program_id(1) == 0))
    def _():
        c = lax.broadcasted_iota(jnp.int32, (GRID_W, 1), 0).astype(F32)
        ang = c * omega
        col_ref[:, :q] = jnp.sin(ang)
        col_ref[:, q:] = jnp.cos(ang)

    r = (i * rows_per_tile + lax.broadcasted_iota(jnp.int32, (rows_per_tile, 1), 0)).astype(F32)
    ang_r = r * omega
    row_pe = jnp.concatenate([jnp.sin(ang_r), jnp.cos(ang_r)], axis=-1)
    for rr in range(rows_per_tile):
        sl = slice(rr * GRID_W, (rr + 1) * GRID_W)
        o_ref[sl, :2 * q] = x_ref[sl, :2 * q] + row_pe[rr:rr + 1, :]
        o_ref[sl, 2 * q:] = x_ref[sl, 2 * q:] + col_ref[...]


def _embed(x):
    b, l, d = x.shape
    rows_per_tile = 8
    tl = rows_per_tile * GRID_W
    return pl.pallas_call(
        functools.partial(_embed_kernel, rows_per_tile=rows_per_tile),
        out_shape=jax.ShapeDtypeStruct(x.shape, F32),
        grid=(l // tl, b),
        in_specs=[pl.BlockSpec((None, tl, d), lambda i, bb: (bb, i, 0))],
        out_specs=pl.BlockSpec((None, tl, d), lambda i, bb: (bb, i, 0)),
        scratch_shapes=[pltpu.VMEM((GRID_W, d // 2), F32)],
        compiler_params=_cparams(("arbitrary", "arbitrary")),
        name="embed",
    )(x)


def _rmsnorm_kernel(h_ref, g_ref, o_ref):
    x = h_ref[...]
    o_ref[...] = x * lax.rsqrt(jnp.mean(x * x, axis=-1, keepdims=True) + EPS) * g_ref[...]


def _final_norm(h, g):
    b, l, d = h.shape
    tm = 512
    return pl.pallas_call(
        _rmsnorm_kernel,
        out_shape=jax.ShapeDtypeStruct(h.shape, F32),
        grid=(b, l // tm),
        in_specs=[pl.BlockSpec((None, tm, d), lambda bb, i: (bb, i, 0)),
                  pl.BlockSpec((1, d), lambda bb, i: (0, 0))],
        out_specs=pl.BlockSpec((None, tm, d), lambda bb, i: (bb, i, 0)),
        compiler_params=_cparams(("arbitrary", "arbitrary")),
        name="final_norm",
    )(h, g.reshape(1, d))


def _w_index(bb, i, j, *, layer, off):
    return (layer, 0, j + off)


def _normmod_matmul_kernel(hf_ref, hp_ref, g_ref, sh_ref, sc_ref, shn_ref, scn_ref, *rest, n_w, mode, n_pieces):
    w_refs = rest[:n_w]
    out_refs = rest[n_w:-2]
    slots = rest[-2:]
    n = pl.program_id(0) * pl.num_programs(1) + pl.program_id(1)
    j = pl.program_id(2)
    g = g_ref[...]

    def normmod(x, sh, sc):
        r = lax.rsqrt(jnp.mean(x * x, axis=-1, keepdims=True) + EPS)
        return (x * r * (g * (1.0 + sc)) + sh).astype(BF16)

    @pl.when((n == 0) & (j == 0))
    def _():
        slots[0][...] = normmod(hf_ref[...], sh_ref[...], sc_ref[...])

    rp = hp_ref.shape[0]
    piece = jnp.minimum(j, n_pieces - 1)

    def step(cur_ref, nxt_ref):
        a = cur_ref[...]
        accs = [jnp.dot(a, w[...], preferred_element_type=F32) for w in w_refs]
        if mode == "plain":
            out_refs[0][...] = accs[0].astype(out_refs[0].dtype)
        elif mode == "gelu_pair":
            out_refs[0][...] = jax.nn.gelu(accs[0]).astype(out_refs[0].dtype)
            out_refs[1][...] = jax.nn.gelu(accs[1]).astype(out_refs[1].dtype)
        elif mode == "swiglu":
            out_refs[0][...] = (jax.nn.silu(accs[0]) * accs[1]).astype(out_refs[0].dtype)
        else:
            raise ValueError(mode)
        nxt_ref[pl.ds(pl.multiple_of(piece * rp, rp), rp), :] = normmod(hp_ref[...], shn_ref[...], scn_ref[...])

    pl.when(n % 2 == 0)(lambda: step(slots[0], slots[1]))
    pl.when(n % 2 == 1)(lambda: step(slots[1], slots[0]))


def _normmod_matmul(h, g, shift, scale, weights, n, mode, out_dtypes, tn, name):
    b, t, d = h.shape
    tm = min(1024, t)
    n_i, n_j = t // tm, n // tn
    n_pieces = min(n_j, 8)
    rp = tm // n_pieces
    assert rp * n_pieces == tm and rp % 16 == 0

    def next_tile(bb, i):
        nn = jnp.minimum(bb * n_i + i + 1, b * n_i - 1)
        return nn // n_i, nn % n_i

    def piece_index(bb, i, j):
        nb, ni = next_tile(bb, i)
        return (nb, ni * n_pieces + jnp.minimum(j, n_pieces - 1), 0)

    def next_mod_index(bb, i, j):
        return (next_tile(bb, i)[0], 0, 0)

    in_specs = [
        pl.BlockSpec((None, tm, d), lambda bb, i, j: (0, 0, 0)),
        pl.BlockSpec((None, rp, d), piece_index),
        pl.BlockSpec((1, d), lambda bb, i, j: (0, 0)),
        pl.BlockSpec((None, 1, d), lambda bb, i, j: (0, 0, 0)),
        pl.BlockSpec((None, 1, d), lambda bb, i, j: (0, 0, 0)),
        pl.BlockSpec((None, 1, d), next_mod_index),
        pl.BlockSpec((None, 1, d), next_mod_index),
    ]
    for _, layer, off in weights:
        in_specs.append(pl.BlockSpec((None, d, tn), functools.partial(_w_index, layer=layer, off=off // tn)))
    out_shape = [jax.ShapeDtypeStruct((b, t, n), dt) for dt in out_dtypes]
    out_specs = [pl.BlockSpec((None, tm, tn), lambda bb, i, j: (bb, i, j)) for _ in out_dtypes]
    return pl.pallas_call(
        functools.partial(_normmod_matmul_kernel, n_w=len(weights), mode=mode, n_pieces=n_pieces),
        out_shape=out_shape,
        grid=(b, n_i, n_j),
        in_specs=in_specs,
        out_specs=out_specs,
        scratch_shapes=[pltpu.VMEM((tm, d), BF16), pltpu.VMEM((tm, d), BF16)],
        compiler_params=_cparams(("arbitrary", "arbitrary", "arbitrary")),
        name=name,
    )(h, h, g.reshape(1, d), shift, scale, shift, scale, *[w for w, _, _ in weights])


def _matmul_res_kernel(a_ref, h_ref, gate_ref, *rest, glu):
    a = a_ref[...]
    acc = jnp.dot(a, rest[0][...], preferred_element_type=F32)
    if glu:
        gl = jnp.dot(a, rest[1][...], preferred_element_type=F32)
        acc = acc * jax.nn.sigmoid(gl)
    o_ref = rest[-1]
    o_ref[...] = h_ref[...] + gate_ref[...] * acc


def _matmul_res(a, w, layer, h, gate, glu, name):
    b, t, k = a.shape
    n = h.shape[-1]
    tm = min(1024, t)
    tn = 512
    in_specs = [
        pl.BlockSpec((None, tm, k), lambda bb, i, j: (bb, i, 0)),
        pl.BlockSpec((None, tm, tn), lambda bb, i, j: (bb, i, j)),
        pl.BlockSpec((None, 1, tn), lambda bb, i, j: (bb, 0, j)),
        pl.BlockSpec((None, k, tn), functools.partial(_w_index, layer=layer, off=0)),
    ]
    args = [a, h, gate, w]
    if glu:
        in_specs.append(pl.BlockSpec((None, k, tn), functools.partial(_w_index, layer=layer, off=n // tn)))
        args.append(w)
    return pl.pallas_call(
        functools.partial(_matmul_res_kernel, glu=glu),
        out_shape=jax.ShapeDtypeStruct(h.shape, F32),
        grid=(b, t // tm, n // tn),
        in_specs=in_specs,
        out_specs=pl.BlockSpec((None, tm, tn), lambda bb, i, j: (bb, i, j)),
        compiler_params=_cparams(("arbitrary", "arbitrary", "arbitrary")),
        name=name,
    )(*args)


def _gmlp_gate_kernel(u_ref, v_ref, lng_ref, lnb_ref, ws_ref, bs_ref, t_ref):
    v = v_ref[...].astype(F32)
    mu = jnp.mean(v, axis=-1, keepdims=True)
    xc = v - mu
    var = jnp.mean(xc * xc, axis=-1, keepdims=True)
    vn = (xc * lax.rsqrt(var + EPS) * lng_ref[...] + lnb_ref[...]).astype(BF16)
    hd = v.shape[-1] // GM_HEADS
    for cc in range(v.shape[0] // CHUNK):
        rows = slice(cc * CHUNK, (cc + 1) * CHUNK)
        for hh in range(GM_HEADS):
            sl = slice(hh * hd, (hh + 1) * hd)
            s = jnp.dot(ws_ref[hh], vn[rows, sl], preferred_element_type=F32) + bs_ref[hh]
            t_ref[rows, sl] = (u_ref[rows, sl].astype(F32) * s).astype(BF16)


def _gmlp_gate(u, v, ln_g, ln_b, w_s, b_s):
    b, t, e = u.shape
    tr = 2 * CHUNK
    return pl.pallas_call(
        _gmlp_gate_kernel,
        out_shape=jax.ShapeDtypeStruct((b, t, e), BF16),
        grid=(b, t // tr),
        in_specs=[
            pl.BlockSpec((None, tr, e), lambda bb, i: (bb, i, 0)),
            pl.BlockSpec((None, tr, e), lambda bb, i: (bb, i, 0)),
            pl.BlockSpec((1, e), lambda bb, i: (0, 0)),
            pl.BlockSpec((1, e), lambda bb, i: (0, 0)),
            pl.BlockSpec((GM_HEADS, CHUNK, CHUNK), lambda bb, i: (0, 0, 0)),
            pl.BlockSpec((GM_HEADS, CHUNK, 1), lambda bb, i: (0, 0, 0)),
        ],
        out_specs=pl.BlockSpec((None, tr, e), lambda bb, i: (bb, i, 0)),
        compiler_params=_cparams(("arbitrary", "arbitrary")),
        name="gmlp_gate",
    )(u, v, ln_g.reshape(1, e), ln_b.reshape(1, e), w_s.astype(BF16), b_s.reshape(GM_HEADS, CHUNK, 1))


def _dot_split(a, b):
    a_hi, b_hi = a.astype(BF16), b.astype(BF16)
    a_lo = (a - a_hi.astype(F32)).astype(BF16)
    b_lo = (b - b_hi.astype(F32)).astype(BF16)
    dot = functools.partial(jnp.dot, preferred_element_type=F32)
    return dot(a_hi, b_hi) + (dot(a_hi, b_lo) + dot(a_lo, b_hi))


def _s5_ops_kernel(lane_ref, bt_ref, ct_ref, min_ref, mintra_ref, mout_ref, dec_ref):
    t = S5_T
    sw = S5_SW
    mask_b =(lax.broadcasted_iota(jnp.int32, (LANES, sw), 0) // S5_GROUP
              == lax.broadcasted_iota(jnp.int32, (LANES, sw), 1) // S5_STATE)
    mask_c = (lax.broadcasted_iota(jnp.int32, (sw, LANES), 0) // S5_STATE
              == lax.broadcasted_iota(jnp.int32, (sw, LANES), 1) // S5_GROUP)

    def to_rows(v):
        return jnp.concatenate(
            [jnp.broadcast_to(v[:, i * LANES:(i + 1) * LANES], (LANES, LANES)).T for i in range(sw // LANES)],
            axis=0)

    lag = []
    for dd in range(2):
        a_re, a_im = lane_ref[dd, 0:1, :], lane_ref[dd, 1:2, :]
        dt = jnp.exp(lane_ref[dd, 2:3, :])
        mag = jnp.exp(a_re * dt)
        p1_re, p1_im = mag * jnp.cos(a_im * dt), mag * jnp.sin(a_im * dt)
        nr, ni = p1_re - 1.0, p1_im
        den = a_re * a_re + a_im * a_im
        cf_re, cf_im = (nr * a_re + ni * a_im) / den, (ni * a_re - nr * a_im) / den
        b_re = jnp.where(mask_b, bt_ref[dd, 0], 0.0)
        b_im = jnp.where(mask_b, bt_ref[dd, 1], 0.0)
        bb_re = cf_re * b_re - cf_im * b_im
        bb_im = cf_re * b_im + cf_im * b_re
        pw = [(jnp.ones_like(p1_re), jnp.zeros_like(p1_re))]
        for _ in range(t):
            pr, pi = pw[-1]
            pw.append((pr * p1_re - pi * p1_im, pr * p1_im + pi * p1_re))
        dec_ref[dd, 0:1, :] = pw[t][0]
        dec_ref[dd, 1:2, :] = pw[t][1]
        ab = [(pr * bb_re - pi * bb_im, pr * bb_im + pi * bb_re) for pr, pi in pw[:t]]
        for s in range(t):
            m_re, m_im = ab[t - 1 - s] if dd == 0 else ab[s]
            min_ref[dd, s * LANES:(s + 1) * LANES, 0:sw] = m_re.astype(BF16)
            min_ref[dd, s * LANES:(s + 1) * LANES, sw:2 * sw] = m_im.astype(BF16)
        r1_re, r1_im = to_rows(p1_re), to_rows(p1_im)
        qr = jnp.where(mask_c, ct_ref[dd, 0], 0.0)
        qi = jnp.where(mask_c, ct_ref[dd, 1], 0.0)
        lag.append(_dot_split(jnp.concatenate([x[0] for x in ab], axis=0), qr)
                   - _dot_split(jnp.concatenate([x[1] for x in ab], axis=0), qi))
        mo = []
        for _ in range(t + 1):
            mo.append((qr, -qi))
            qr, qi = qr * r1_re - qi * r1_im, qr * r1_im + qi * r1_re
        for tt in range(t):
            m_re, m_im = mo[tt + 1] if dd == 0 else mo[t - tt]
            mout_ref[dd, 0:sw, tt * LANES:(tt + 1) * LANES] = m_re.astype(BF16)
            mout_ref[dd, sw:2 * sw, tt * LANES:(tt + 1) * LANES] = m_im.astype(BF16)

    def lag_blk(dd, k):
        return lag[dd][k * LANES:(k + 1) * LANES, :]

    for s in range(t):
        for tt in range(t):
            if s < tt:
                blk = lag_blk(0, tt - s)
            elif s > tt:
                blk = lag_blk(1, s - tt)
            else:
                blk = lag_blk(0, 0) + lag_blk(1, 0)
            mintra_ref[s * LANES:(s + 1) * LANES, tt * LANES:(tt + 1) * LANES] = blk.astype(BF16)


def _s5_ops(a_re, a_im, log_dt, b_re, b_im, c_re, c_im):
    nl, _, n_g, p = a_re.shape
    q = b_re.shape[-1]
    n_blk = n_g // S5_GB
    nb = nl * n_blk
    lane = jnp.stack([a_re, a_im, jnp.broadcast_to(log_dt[..., None], a_re.shape)], axis=2)
    lane = jnp.transpose(lane.reshape(nl, 2, 3, n_blk, S5_SW), (0, 3, 1, 2, 4)).reshape(nb, 2, 3, S5_SW)
    bt = jnp.stack([b_re, b_im], axis=2)
    bt = jnp.swapaxes(bt, -1, -2).reshape(nl, 2, 2, n_blk, S5_GB * q, p)
    bt = jnp.tile(jnp.transpose(bt, (0, 3, 1, 2, 4, 5)), (1, 1, 1, 1, 1, S5_GB)).reshape(nb, 2, 2, LANES, S5_SW)
    ct = jnp.stack([c_re, c_im], axis=2)
    ct = jnp.swapaxes(ct, -1, -2).reshape(nl, 2, 2, n_blk, S5_SW, q)
    ct = jnp.tile(jnp.transpose(ct, (0, 3, 1, 2, 4, 5)), (1, 1, 1, 1, 1, S5_GB)).reshape(nb, 2, 2, S5_SW, LANES)
    return pl.pallas_call(
        _s5_ops_kernel,
        out_shape=[
            jax.ShapeDtypeStruct((nb, 2, S5_KW, 2 * S5_SW), BF16),
            jax.ShapeDtypeStruct((nb, S5_KW, S5_KW), BF16),
            jax.ShapeDtypeStruct((nb, 2, 2 * S5_SW, S5_KW), BF16),
            jax.ShapeDtypeStruct((nb, 2, 2, S5_SW), F32),
        ],
        grid=(nb,),
        in_specs=[
            pl.BlockSpec((None, 2, 3, S5_SW), lambda j: (j, 0, 0, 0)),
            pl.BlockSpec((None, 2, 2, LANES, S5_SW), lambda j: (j, 0, 0, 0, 0)),
            pl.BlockSpec((None, 2, 2, S5_SW, LANES), lambda j: (j, 0, 0, 0, 0)),
        ],
        out_specs=[
            pl.BlockSpec((None, 2, S5_KW, 2 * S5_SW), lambda j: (j, 0, 0, 0)),
            pl.BlockSpec((None, S5_KW, S5_KW), lambda j: (j, 0, 0)),
            pl.BlockSpec((None, 2, 2 * S5_SW, S5_KW), lambda j: (j, 0, 0, 0)),
            pl.BlockSpec((None, 2, 2, S5_SW), lambda j: (j, 0, 0, 0)),
        ],
        compiler_params=_cparams(("arbitrary",)),
        name="s5_ops",
    )(lane, bt, ct)


def _s5_kernel(uc_ref, ul_ref, d_ref, min_ref, mintra_ref, mout_ref, dec_ref, zc_ref, zl_ref,
               ucb_ref, xh_ref, y_ref, *, nc_ctx, nc_lat, row_blk):
    nch = nc_ctx + nc_lat
    t = S5_T
    for tt in range(t):
        ucb_ref[0:nc_ctx, tt * LANES:(tt + 1) * LANES] = uc_ref[pl.ds(tt, nc_ctx, stride=t), :].astype(BF16)
        ucb_ref[nc_ctx:nch, tt * LANES:(tt + 1) * LANES] = ul_ref[pl.ds(tt, nc_lat, stride=t), :].astype(BF16)

    n_rb = nch // row_blk
    nsl = 2 * S5_SW // LANES
    ncs = nsl // 2

    def row_pieces(dd, lo, hi):
        if dd == 0:
            return [(lo, hi, lo)]
        out = []
        if lo < nc_ctx:
            out.append((lo, min(hi, nc_ctx), lo + nc_lat))
        if hi > nc_ctx:
            out.append((max(lo, nc_ctx), hi, max(lo, nc_ctx) - nc_ctx))
        return out

    for dd in range(2):
        for rb in range(n_rb):
            lo, hi = rb * row_blk, (rb + 1) * row_blk
            x = jnp.dot(ucb_ref[lo:hi, :], min_ref[dd], preferred_element_type=F32)
            for s0, s1, d0 in row_pieces(dd, lo, hi):
                for k in range(nsl):
                    xh_ref[dd, k, d0:d0 + s1 - s0, :] = x[s0 - lo:s1 - lo, k * LANES:(k + 1) * LANES]

    seg = nch // 8

    def start_row(dd, i):
        return i if dd == 0 else seg - 1 - i

    def dec_slab(dd, k, ri):
        return dec_ref[dd, ri:ri + 1, k * LANES:(k + 1) * LANES]

    dec_b = [[(jnp.broadcast_to(dec_slab(dd, k, 0), (8, LANES)), jnp.broadcast_to(dec_slab(dd, k, 1), (8, LANES)))
              for k in range(ncs)] for dd in range(2)]

    def local_scan(i, st):
        new = []
        for dd in range(2):
            rows = pl.ds(start_row(dd, i), 8, stride=seg)
            for k in range(ncs):
                lr, li = st[2 * (dd * ncs + k)], st[2 * (dd * ncs + k) + 1]
                re_ref, im_ref = xh_ref.at[dd, k], xh_ref.at[dd, ncs + k]
                xr, xi = re_ref[rows, :], im_ref[rows, :]
                re_ref[rows, :] = lr
                im_ref[rows, :] = li
                ar, ai = dec_b[dd][k]
                new += [ar * lr - ai * li + xr, ar * li + ai * lr + xi]
        return tuple(new)

    zero = jnp.zeros((8, LANES), F32)
    ends = lax.fori_loop(0, seg, local_scan, (zero,) * (4 * ncs), unroll=2)

    def cmul(a, b):
        return a[0] * b[0] - a[1] * b[1], a[0] * b[1] + a[1] * b[0]

    def cpow(a, e):
        acc, sq = None, a
        while e:
            if e & 1:
                acc = sq if acc is None else cmul(acc, sq)
            sq = cmul(sq, sq)
            e >>= 1
        return acc

    carries = []
    for dd in range(2):
        order = list(range(8)) if dd == 0 else list(range(7, -1, -1))
        for k in range(ncs):
            a_seg = cpow((dec_slab(dd, k, 0), dec_slab(dd, k, 1)), seg)
            lr, li = ends[2 * (dd * ncs + k)], ends[2 * (dd * ncs + k) + 1]
            c = (jnp.zeros((1, LANES), F32), jnp.zeros((1, LANES), F32))
            by_sublane = {}
            for m in order:
                by_sublane[m] = c
                pr, pi = cmul(a_seg, c)
                c = (pr + lr[m:m + 1, :], pi + li[m:m + 1, :])
            carries.append(jnp.concatenate([by_sublane[m][0] for m in range(8)], axis=0))
            carries.append(jnp.concatenate([by_sublane[m][1] for m in range(8)], axis=0))

    def add_carry(i, gs):
        new = []
        for dd in range(2):
            rows = pl.ds(start_row(dd, i), 8, stride=seg)
            for k in range(ncs):
                gr, gi = gs[2 * (dd * ncs + k)], gs[2 * (dd * ncs + k) + 1]
                re_ref, im_ref = xh_ref.at[dd, k], xh_ref.at[dd, ncs + k]
                re_ref[rows, :] = re_ref[rows, :] + gr
                im_ref[rows, :] = im_ref[rows, :] + gi
                ar, ai = dec_b[dd][k]
                new += [ar * gr - ai * gi, ar * gi + ai * gr]
        return tuple(new)

    lax.fori_loop(0, seg, add_carry, tuple(carries), unroll=2)

    def state_rows(dd, lo, hi):
        parts = [jnp.concatenate([xh_ref[dd, k, d0:d0 + s1 - s0, :] for k in range(nsl)], axis=-1)
                 for s0, s1, d0 in row_pieces(dd, lo, hi)]
        return (parts[0] if len(parts) == 1 else jnp.concatenate(parts, axis=0)).astype(BF16)

    for rb in range(n_rb):
        lo, hi = rb * row_blk, (rb + 1) * row_blk
        y = jnp.dot(ucb_ref[lo:hi, :], mintra_ref[...], preferred_element_type=F32)
        y = y + jnp.dot(state_rows(0, lo, hi), mout_ref[0], preferred_element_type=F32)
        y = y + jnp.dot(state_rows(1, lo, hi), mout_ref[1], preferred_element_type=F32)
        for tt in range(t):
            y_ref[pl.ds(rb * row_blk * t + tt, row_blk, stride=t), :] = y[:, tt * LANES:(tt + 1) * LANES]

    ctx_len = nc_ctx * t
    dv = d_ref[...]
    zc_ref[...] = jax.nn.gelu(y_ref[0:ctx_len, :] + uc_ref[...] * dv).astype(zc_ref.dtype)
    zl_ref[...] = jax.nn.gelu(y_ref[ctx_len:, :] + ul_ref[...] * dv).astype(zl_ref.dtype)


def _s5_core(u_ctx, u_lat, d_skip, ops, layer):
    m_in, m_intra, m_out, decay = ops
    b, l, w = u_lat.shape
    ctx_len = u_ctx.shape[1]
    n_blk = w // LANES
    nc_ctx, nc_lat = ctx_len // S5_T, l // S5_T
    nch = nc_ctx + nc_lat
    row_blk = nch // 4
    assert row_blk % 8 == 0 and row_blk * 4 == nch and nc_ctx % 16 == 0 and nch % 8 == 0
    kern = functools.partial(_s5_kernel, nc_ctx=nc_ctx, nc_lat=nc_lat, row_blk=row_blk)
    base = layer * n_blk
    return pl.pallas_call(
        kern,
        out_shape=[jax.ShapeDtypeStruct(u_ctx.shape, BF16), jax.ShapeDtypeStruct(u_lat.shape, BF16)],
        grid=(n_blk, b),
        in_specs=[
            pl.BlockSpec((None, ctx_len, LANES), lambda j, bb: (bb, 0, j)),
            pl.BlockSpec((None, l, LANES), lambda j, bb: (bb, 0, j)),
            pl.BlockSpec((None, 1, LANES), lambda j, bb: (layer, 0, j)),
            pl.BlockSpec((None, 2, S5_KW, 2 * S5_SW), lambda j, bb: (base + j, 0, 0, 0)),
            pl.BlockSpec((None, S5_KW, S5_KW), lambda j, bb: (base + j, 0, 0)),
            pl.BlockSpec((None, 2, 2 * S5_SW, S5_KW), lambda j, bb: (base + j, 0, 0, 0)),
            pl.BlockSpec((None, 2, 2, S5_SW), lambda j, bb: (base + j, 0, 0, 0)),
        ],
        out_specs=[
            pl.BlockSpec((None, ctx_len, LANES), lambda j, bb: (bb, 0, j)),
            pl.BlockSpec((None, l, LANES), lambda j, bb: (bb, 0, j)),
        ],
        scratch_shapes=[
            pltpu.VMEM((nch, S5_KW), BF16),
            pltpu.VMEM((2, 2 * S5_SW // LANES, nch, LANES), F32),
            pltpu.VMEM((nch * S5_T, LANES), F32),
        ],
        compiler_params=_cparams(("arbitrary", "arbitrary")),
        name="s5_core",
    )(u_ctx, u_lat, d_skip.reshape(d_skip.shape[0], 1, w), m_in, m_intra, m_out, decay)


def kernel(x, c, ctx, c_ctx, ada_w, ada_b, norm1_g, norm2_g, ffn_w1, ffn_w3, ffn_w2, gm_w_in, gm_ln_g, gm_ln_b, gm_w_s, gm_b_s, gm_w_out, s5_w_in, s5_a_re, s5_a_im, s5_log_dt, s5_b_re, s5_b_im, s5_c_re, s5_c_im, s5_d, s5_w_glu, final_g):
    bsz, seq, d = x.shape
    depth = ada_w.shape[0]
    ffn_hidden = ffn_w1.shape[-1]
    gm_width = gm_w_out.shape[1]
    s5_layers = [i for i in range(depth) if i % N_MIXERS == 1]
    last_s5 = s5_layers[-1] if s5_layers else -1

    cond_rows = jnp.concatenate([c, c_ctx[None, :]], axis=0)
    mod = _adaln_all(cond_rows, ada_w, ada_b)

    def mods(i, stream):
        if stream == "lat":
            m = mod[i, :bsz]
        else:
            m = jnp.broadcast_to(mod[i, bsz:bsz + 1], (bsz, 6 * d))
        return [m[:, None, k * d:(k + 1) * d] for k in range(6)]

    w1, w3, w2 = ffn_w1.astype(BF16), ffn_w3.astype(BF16), ffn_w2.astype(BF16)
    gw_in, gw_out = gm_w_in.astype(BF16), gm_w_out.astype(BF16)
    sw_in, sw_glu = s5_w_in.astype(BF16), s5_w_glu.astype(BF16)
    s5_ops = _s5_ops(s5_a_re, s5_a_im, s5_log_dt, s5_b_re, s5_b_im, s5_c_re, s5_c_im) if s5_layers else None

    h = _embed(x)
    hc = ctx

    for i in range(depth):
        ctx_read = i <= last_s5
        ctx_carry = i < last_s5
        j = i // N_MIXERS
        streams = [("lat", h)]
        if ctx_read:
            streams.append(("ctx", hc))
        new = {}
        if i % N_MIXERS == 0:
            for name, hs in streams:
                if name == "ctx" and not ctx_carry:
                    continue
                sh1, sc1, g1, _, _, _ = mods(i, name)
                u, v = _normmod_matmul(hs, norm1_g[i], sh1, sc1, [(gw_in, j, 0), (gw_in, j, gm_width)],
                                       gm_width, "gelu_pair", [BF16, BF16], 512, "gm_in")
                tg = _gmlp_gate(u, v, gm_ln_g[j], gm_ln_b[j], gm_w_s[j], gm_b_s[j])
                new[name] = _matmul_res(tg, gw_out, j, hs, g1, False, "gm_out")
        else:
            us = {}
            for name, hs in streams:
                sh1, sc1, _, _, _, _ = mods(i, name)
                us[name] = _normmod_matmul(hs, norm1_g[i], sh1, sc1, [(sw_in, j, 0)],
                                           d, "plain", [F32], 512, "s5_in")[0]
            z_ctx, z_lat = _s5_core(us["ctx"], us["lat"], s5_d, s5_ops, j)
            new["lat"] = _matmul_res(z_lat, sw_glu, j, h, mods(i, "lat")[2], True, "s5_out")
            if ctx_carry:
                new["ctx"] = _matmul_res(z_ctx, sw_glu, j, hc, mods(i, "ctx")[2], True, "s5_out")
        h = new["lat"]
        if ctx_carry:
            hc = new["ctx"]

        streams = [("lat", h)] + ([("ctx", hc)] if ctx_carry else [])
        for name, hs in streams:
            _, _, _, sh2, sc2, g2 = mods(i, name)
            tf = _normmod_matmul(hs, norm2_g[i], sh2, sc2, [(w1, i, 0), (w3, i, 0)],
                                 ffn_hidden, "swiglu", [BF16], 512, "ffn_in")[0]
            out = _matmul_res(tf, w2, i, hs, g2, False, "ffn_out")
            if name == "lat":
                h = out
            else:
                hc = out

    return _final_norm(h, final_g)
```

```python
import functools
import math

import jax
import jax.numpy as jnp
from jax import lax
from jax.experimental import pallas as pl
from jax.experimental.pallas import tpu as pltpu

F32 = jnp.float32
BF16 = jnp.bfloat16

EPS = 1e-6
GRID_W = 64
CHUNK = 128
GM_HEADS = 16
S5_GROUP = 16
S5_STATE = 64
N_MIXERS = 2

LANES = 128
S5_T = 8
S5_GB = LANES // S5_GROUP
S5_KW = S5_T * LANES
S5_SW = S5_GB * S5_STATE
VMEM_LIMIT = 56 * 1024 * 1024


def _cparams(sem):
    return pltpu.CompilerParams(dimension_semantics=sem, vmem_limit_bytes=VMEM_LIMIT)


def _adaln_kernel(cond_ref, w_ref, b_ref, o_ref, s_ref, *, n_rows, kc):
    x = cond_ref[...]
    s_ref[...] = x * jax.nn.sigmoid(x)
    d, tn = w_ref.shape
    nslab = tn // LANES

    def body(kk, accs):
        k0 = pl.multiple_of(kk * kc, kc)
        new = []
        for r in range(n_rows):
            sb = s_ref[r, pl.ds(k0, kc), :]
            for n in range(nslab):
                w = w_ref[pl.ds(k0, kc), n * LANES:(n + 1) * LANES]
                new.append(accs[r * nslab + n] + (w * sb).reshape(kc // 8, 8, LANES).sum(axis=0))
        return tuple(new)

    init = tuple(jnp.zeros((8, LANES), F32) for _ in range(n_rows * nslab))
    accs = lax.fori_loop(0, d // kc, body, init)
    o_ref[...] = jnp.zeros(o_ref.shape, F32)
    for r in range(n_rows):
        for n in range(nslab):
            o_ref[r:r + 1, n * LANES:(n + 1) * LANES] = (
                jnp.sum(accs[r * nslab + n], axis=0, keepdims=True) + b_ref[:, n * LANES:(n + 1) * LANES])


def _adaln_all(cond_rows, ada_w, ada_b):
    n_rows, d = cond_rows.shape
    depth, _, n6 = ada_w.shape
    tn = 1536
    cond_b =jnp.broadcast_to(cond_rows[:, :, None], (n_rows, d, LANES))
    return pl.pallas_call(
        functools.partial(_adaln_kernel, n_rows=n_rows, kc=64),
        out_shape=jax.ShapeDtypeStruct((depth, 8, n6), F32),
        grid=(depth, n6 // tn),
        in_specs=[
            pl.BlockSpec((n_rows, d, LANES), lambda l, j: (0, 0, 0)),
            pl.BlockSpec((None, d, tn), lambda l, j: (l, 0, j)),
            pl.BlockSpec((None, 1, tn), lambda l, j: (l, 0, j)),
        ],
        out_specs=pl.BlockSpec((None, 8, tn), lambda l, j: (l, 0, j)),
        scratch_shapes=[pltpu.VMEM((n_rows, d, LANES), F32)],
        compiler_params=_cparams(("arbitrary", "arbitrary")),
        name="adaln",
    )(cond_b, ada_w, ada_b.reshape(depth, 1, n6))


def _embed_kernel(x_ref, o_ref, col_ref, *, rows_per_tile):
    d = x_ref.shape[-1]
    q = d // 4
    k = lax.broadcasted_iota(jnp.int32, (1, q), 1).astype(F32)
    omega = jnp.exp(k * (-math.log(10000.0) / q))
    i = pl.program_id(0)

    @pl.when((i == 0) & (pl.program_id(1) == 0))
    def _():
        c = lax.broadcasted_iota(jnp.int32, (GRID_W, 1), 0).astype(F32)
        ang = c * omega
        col_ref[:, :q] = jnp.sin(ang)
        col_ref[:, q:] = jnp.cos(ang)

    r = (i * rows_per_tile + lax.broadcasted_iota(jnp.int32, (rows_per_tile, 1), 0)).astype(F32)
    ang_r = r * omega
    row_pe = jnp.concatenate([jnp.sin(ang_r), jnp.cos(ang_r)], axis=-1)
    for rr in range(rows_per_tile):
        sl = slice(rr * GRID_W, (rr + 1) * GRID_W)
        o_ref[sl, :2 * q] = x_ref[sl, :2 * q] + row_pe[rr:rr + 1, :]
        o_ref[sl, 2 * q:] = x_ref[sl, 2 * q:] + col_ref[...]


def _embed(x):
    b, l, d = x.shape
    rows_per_tile = 8
    tl = rows_per_tile * GRID_W
    return pl.pallas_call(
        functools.partial(_embed_kernel, rows_per_tile=rows_per_tile),
        out_shape=jax.ShapeDtypeStruct(x.shape, F32),
        grid=(l // tl, b),
        in_specs=[pl.BlockSpec((None, tl, d), lambda i, bb: (bb, i, 0))],
        out_specs=pl.BlockSpec((None, tl, d), lambda i, bb: (bb, i, 0)),
        scratch_shapes=[pltpu.VMEM((GRID_W, d // 2), F32)],
        compiler_params=_cparams(("arbitrary", "arbitrary")),
        name="embed",
    )(x)


def _rmsnorm_kernel(h_ref, g_ref, o_ref):
    x = h_ref[...]
    o_ref[...] = x * lax.rsqrt(jnp.mean(x * x, axis=-1, keepdims=True) + EPS) * g_ref[...]


def _final_norm(h, g):
    b, l, d = h.shape
    tm = 512
    return pl.pallas_call(
        _rmsnorm_kernel,
        out_shape=jax.ShapeDtypeStruct(h.shape, F32),
        grid=(b, l // tm),
        in_specs=[pl.BlockSpec((None, tm, d), lambda bb, i: (bb, i, 0)),
                  pl.BlockSpec((1, d), lambda bb, i: (0, 0))],
        out_specs=pl.BlockSpec((None, tm, d), lambda bb, i: (bb, i, 0)),
        compiler_params=_cparams(("arbitrary", "arbitrary")),
        name="final_norm",
    )(h, g.reshape(1, d))


def _w_index(bb, i, j, *, layer, off):
    return (layer, 0, j + off)


def _normmod_matmul_kernel(hf_ref, hp_ref, g_ref, sh_ref, sc_ref, shn_ref, scn_ref, *rest, n_w, mode, n_pieces):
    w_refs = rest[:n_w]
    out_refs = rest[n_w:-2]
    slots = rest[-2:]
    n = pl.program_id(0) * pl.num_programs(1) + pl.program_id(1)
    j = pl.program_id(2)
    g = g_ref[...]

    def normmod(x, sh, sc):
        r = lax.rsqrt(jnp.mean(x * x, axis=-1, keepdims=True) + EPS)
        return (x * r * (g * (1.0 + sc)) + sh).astype(BF16)

    @pl.when((n == 0) & (j == 0))
    def _():
        slots[0][...] = normmod(hf_ref[...], sh_ref[...], sc_ref[...])

    rp = hp_ref.shape[0]
    piece = jnp.minimum(j, n_pieces - 1)

    def step(cur_ref, nxt_ref):
        a = cur_ref[...]
        accs = [jnp.dot(a, w[...], preferred_element_type=F32) for w in w_refs]
        if mode == "plain":
            out_refs[0][...] = accs[0].astype(out_refs[0].dtype)
        elif mode == "gelu_pair":
            out_refs[0][...] = jax.nn.gelu(accs[0]).astype(out_refs[0].dtype)
            out_refs[1][...] = jax.nn.gelu(accs[1]).astype(out_refs[1].dtype)
        elif mode == "swiglu":
            out_refs[0][...] = (jax.nn.silu(accs[0]) * accs[1]).astype(out_refs[0].dtype)
        else:
            raise ValueError(mode)
        nxt_ref[pl.ds(pl.multiple_of(piece * rp, rp), rp), :] = normmod(hp_ref[...], shn_ref[...], scn_ref[...])

    pl.when(n % 2 == 0)(lambda: step(slots[0], slots[1]))
    pl.when(n % 2 == 1)(lambda: step(slots[1], slots[0]))


def _normmod_matmul(h, g, shift, scale, weights, n, mode, out_dtypes, tn, name):
    b, t, d = h.shape
    tm = min(1024, t)
    n_i, n_j = t // tm, n // tn
    n_pieces = min(n_j, 8)
    rp = tm // n_pieces
    assert rp * n_pieces == tm and rp % 16 == 0

    def next_tile(bb, i):
        nn = jnp.minimum(bb * n_i + i + 1, b * n_i - 1)
        return nn // n_i, nn % n_i

    def piece_index(bb, i, j):
        nb, ni = next_tile(bb, i)
        return (nb, ni * n_pieces + jnp.minimum(j, n_pieces - 1), 0)

    def next_mod_index(bb, i, j):
        return (next_tile(bb, i)[0], 0, 0)

    in_specs = [
        pl.BlockSpec((None, tm, d), lambda bb, i, j: (0, 0, 0)),
        pl.BlockSpec((None, rp, d), piece_index),
        pl.BlockSpec((1, d), lambda bb, i, j: (0, 0)),
        pl.BlockSpec((None, 1, d), lambda bb, i, j: (0, 0, 0)),
        pl.BlockSpec((None, 1, d), lambda bb, i, j: (0, 0, 0)),
        pl.BlockSpec((None, 1, d), next_mod_index),
        pl.BlockSpec((None, 1, d), next_mod_index),
    ]
    for _, layer, off in weights:
        in_specs.append(pl.BlockSpec((None, d, tn), functools.partial(_w_index, layer=layer, off=off // tn)))
    out_shape = [jax.ShapeDtypeStruct((b, t, n), dt) for dt in out_dtypes]
    out_specs = [pl.BlockSpec((None, tm, tn), lambda bb, i, j: (bb, i, j)) for _ in out_dtypes]
    return pl.pallas_call(
        functools.partial(_normmod_matmul_kernel, n_w=len(weights), mode=mode, n_pieces=n_pieces),
        out_shape=out_shape,
        grid=(b, n_i, n_j),
        in_specs=in_specs,
        out_specs=out_specs,
        scratch_shapes=[pltpu.VMEM((tm, d), BF16), pltpu.VMEM((tm, d), BF16)],
        compiler_params=_cparams(("arbitrary", "arbitrary", "arbitrary")),
        name=name,
    )(h, h, g.reshape(1, d), shift, scale, shift, scale, *[w for w, _, _ in weights])


def _matmul_res_kernel(a_ref, h_ref, gate_ref, *rest, glu):
    a = a_ref[...]
    acc = jnp.dot(a, rest[0][...], preferred_element_type=F32)
    if glu:
        gl = jnp.dot(a, rest[1][...], preferred_element_type=F32)
        acc = acc * jax.nn.sigmoid(gl)
    o_ref = rest[-1]
    o_ref[...] = h_ref[...] + gate_ref[...] * acc


def _matmul_res(a, w, layer, h, gate, glu, name):
    b, t, k = a.shape
    n = h.shape[-1]
    tm = min(1024, t)
    tn = 512
    in_specs = [
        pl.BlockSpec((None, tm, k), lambda bb, i, j: (bb, i, 0)),
        pl.BlockSpec((None, tm, tn), lambda bb, i, j: (bb, i, j)),
        pl.BlockSpec((None, 1, tn), lambda bb, i, j: (bb, 0, j)),
        pl.BlockSpec((None, k, tn), functools.partial(_w_index, layer=layer, off=0)),
    ]
    args = [a, h, gate, w]
    if glu:
        in_specs.append(pl.BlockSpec((None, k, tn), functools.partial(_w_index, layer=layer, off=n // tn)))
        args.append(w)
    return pl.pallas_call(
        functools.partial(_matmul_res_kernel, glu=glu),
        out_shape=jax.ShapeDtypeStruct(h.shape, F32),
        grid=(b, t // tm, n // tn),
        in_specs=in_specs,
        out_specs=pl.BlockSpec((None, tm, tn), lambda bb, i, j: (bb, i, j)),
        compiler_params=_cparams(("arbitrary", "arbitrary", "arbitrary")),
        name=name,
    )(*args)


def _gmlp_gate_kernel(u_ref, v_ref, lng_ref, lnb_ref, ws_ref, bs_ref, t_ref):
    v = v_ref[...].astype(F32)
    mu = jnp.mean(v, axis=-1, keepdims=True)
    xc = v - mu
    var = jnp.mean(xc * xc, axis=-1, keepdims=True)
    vn = (xc * lax.rsqrt(var + EPS) * lng_ref[...] + lnb_ref[...]).astype(BF16)
    hd = v.shape[-1] // GM_HEADS
    for cc in range(v.shape[0] // CHUNK):
        rows = slice(cc * CHUNK, (cc + 1) * CHUNK)
        for hh in range(GM_HEADS):
            sl = slice(hh * hd, (hh + 1) * hd)
            s = jnp.dot(ws_ref[hh], vn[rows, sl], preferred_element_type=F32) + bs_ref[hh]
            t_ref[rows, sl] = (u_ref[rows, sl].astype(F32) * s).astype(BF16)


def _gmlp_gate(u, v, ln_g, ln_b, w_s, b_s):
    b, t, e = u.shape
    tr = 2 * CHUNK
    return pl.pallas_call(
        _gmlp_gate_kernel,
        out_shape=jax.ShapeDtypeStruct((b, t, e), BF16),
        grid=(b, t // tr),
        in_specs=[
            pl.BlockSpec((None, tr, e), lambda bb, i: (bb, i, 0)),
            pl.BlockSpec((None, tr, e), lambda bb, i: (bb, i, 0)),
            pl.BlockSpec((1, e), lambda bb, i: (0, 0)),
            pl.BlockSpec((1, e), lambda bb, i: (0, 0)),
            pl.BlockSpec((GM_HEADS, CHUNK, CHUNK), lambda bb, i: (0, 0, 0)),
            pl.BlockSpec((GM_HEADS, CHUNK, 1), lambda bb, i: (0, 0, 0)),
        ],
        out_specs=pl.BlockSpec((None, tr, e), lambda bb, i: (bb, i, 0)),
        compiler_params=_cparams(("arbitrary", "arbitrary")),
        name="gmlp_gate",
    )(u, v, ln_g.reshape(1, e), ln_b.reshape(1, e), w_s.astype(BF16), b_s.reshape(GM_HEADS, CHUNK, 1))


def _dot_split(a, b):
    a_hi, b_hi = a.astype(BF16), b.astype(BF16)
    a_lo = (a - a_hi.astype(F32)).astype(BF16)
    b_lo = (b - b_hi.astype(F32)).astype(BF16)
    dot = functools.partial(jnp.dot, preferred_element_type=F32)
    return dot(a_hi, b_hi) + (dot(a_hi, b_lo) + dot(a_lo, b_hi))


def _s5_ops_kernel(lane_ref, bt_ref, ct_ref, min_ref, mintra_ref, mout_ref, dec_ref):
    t = S5_T
    sw = S5_SW
    mask_b =(lax.broadcasted_iota(jnp.int32, (LANES, sw), 0) // S5_GROUP
              == lax.broadcasted_iota(jnp.int32, (LANES, sw), 1) // S5_STATE)
    mask_c = (lax.broadcasted_iota(jnp.int32, (sw, LANES), 0) // S5_STATE
              == lax.broadcasted_iota(jnp.int32, (sw, LANES), 1) // S5_GROUP)

    def to_rows(v):
        return jnp.concatenate(
            [jnp.broadcast_to(v[:, i * LANES:(i + 1) * LANES], (LANES, LANES)).T for i in range(sw // LANES)],
            axis=0)

    lag = []
    for dd in range(2):
        a_re, a_im = lane_ref[dd, 0:1, :], lane_ref[dd, 1:2, :]
        dt = jnp.exp(lane_ref[dd, 2:3, :])
        mag = jnp.exp(a_re * dt)
        p1_re, p1_im = mag * jnp.cos(a_im * dt), mag * jnp.sin(a_im * dt)
        nr, ni = p1_re - 1.0, p1_im
        den = a_re * a_re + a_im * a_im
        cf_re, cf_im = (nr * a_re + ni * a_im) / den, (ni * a_re - nr * a_im) / den
        b_re = jnp.where(mask_b, bt_ref[dd, 0], 0.0)
        b_im = jnp.where(mask_b, bt_ref[dd, 1], 0.0)
        bb_re = cf_re * b_re - cf_im * b_im
        bb_im = cf_re * b_im + cf_im * b_re
        pw = [(jnp.ones_like(p1_re), jnp.zeros_like(p1_re))]
        for _ in range(t):
            pr, pi = pw[-1]
            pw.append((pr * p1_re - pi * p1_im, pr * p1_im + pi * p1_re))
        dec_ref[dd, 0:1, :] = pw[t][0]
        dec_ref[dd, 1:2, :] = pw[t][1]
        ab = [(pr * bb_re - pi * bb_im, pr * bb_im + pi * bb_re) for pr, pi in pw[:t]]
        for s in range(t):
            m_re, m_im = ab[t - 1 - s] if dd == 0 else ab[s]
            min_ref[dd, s * LANES:(s + 1) * LANES, 0:sw] = m_re.astype(BF16)
            min_ref[dd, s * LANES:(s + 1) * LANES, sw:2 * sw] = m_im.astype(BF16)
        r1_re, r1_im = to_rows(p1_re), to_rows(p1_im)
        qr = jnp.where(mask_c, ct_ref[dd, 0], 0.0)
        qi = jnp.where(mask_c, ct_ref[dd, 1], 0.0)
        lag.append(_dot_split(jnp.concatenate([x[0] for x in ab], axis=0), qr)
                   - _dot_split(jnp.concatenate([x[1] for x in ab], axis=0), qi))
        mo = []
        for _ in range(t + 1):
            mo.append((qr, -qi))
            qr, qi = qr * r1_re - qi * r1_im, qr * r1_im + qi * r1_re
        for tt in range(t):
            m_re, m_im = mo[tt + 1] if dd == 0 else mo[t - tt]
            mout_ref[dd, 0:sw, tt * LANES:(tt + 1) * LANES] = m_re.astype(BF16)
            mout_ref[dd, sw:2 * sw, tt * LANES:(tt + 1) * LANES] = m_im.astype(BF16)

    def lag_blk(dd, k):
        return lag[dd][k * LANES:(k + 1) * LANES, :]

    for s in range(t):
        for tt in range(t):
            if s < tt:
                blk = lag_blk(0, tt - s)
            elif s > tt:
                blk = lag_blk(1, s - tt)
            else:
                blk = lag_blk(0, 0) + lag_blk(1, 0)
            mintra_ref[s * LANES:(s + 1) * LANES, tt * LANES:(tt + 1) * LANES] = blk.astype(BF16)


def _s5_ops(a_re, a_im, log_dt, b_re, b_im, c_re, c_im):
    nl, _, n_g, p = a_re.shape
    q = b_re.shape[-1]
    n_blk = n_g // S5_GB
    nb = nl * n_blk
    lane = jnp.stack([a_re, a_im, jnp.broadcast_to(log_dt[..., None], a_re.shape)], axis=2)
    lane = jnp.transpose(lane.reshape(nl, 2, 3, n_blk, S5_SW), (0, 3, 1, 2, 4)).reshape(nb, 2, 3, S5_SW)
    bt = jnp.stack([b_re, b_im], axis=2)
    bt = jnp.swapaxes(bt, -1, -2).reshape(nl, 2, 2, n_blk, S5_GB * q, p)
    bt = jnp.tile(jnp.transpose(bt, (0, 3, 1, 2, 4, 5)), (1, 1, 1, 1, 1, S5_GB)).reshape(nb, 2, 2, LANES, S5_SW)
    ct = jnp.stack([c_re, c_im], axis=2)
    ct = jnp.swapaxes(ct, -1, -2).reshape(nl, 2, 2, n_blk, S5_SW, q)
    ct = jnp.tile(jnp.transpose(ct, (0, 3, 1, 2, 4, 5)), (1, 1, 1, 1, 1, S5_GB)).reshape(nb, 2, 2, S5_SW, LANES)
    return pl.pallas_call(
        _s5_ops_kernel,
        out_shape=[
            jax.ShapeDtypeStruct((nb, 2, S5_KW, 2 * S5_SW), BF16),
            jax.ShapeDtypeStruct((nb, S5_KW, S5_KW), BF16),
            jax.ShapeDtypeStruct((nb, 2, 2 * S5_SW, S5_KW), BF16),
            jax.ShapeDtypeStruct((nb, 2, 2, S5_SW), F32),
        ],
        grid=(nb,),
        in_specs=[
            pl.BlockSpec((None, 2, 3, S5_SW), lambda j: (j, 0, 0, 0)),
            pl.BlockSpec((None, 2, 2, LANES, S5_SW), lambda j: (j, 0, 0, 0, 0)),
            pl.BlockSpec((None, 2, 2, S5_SW, LANES), lambda j: (j, 0, 0, 0, 0)),
        ],
        out_specs=[
            pl.BlockSpec((None, 2, S5_KW, 2 * S5_SW), lambda j: (j, 0, 0, 0)),
            pl.BlockSpec((None, S5_KW, S5_KW), lambda j: (j, 0, 0)),
            pl.BlockSpec((None, 2, 2 * S5_SW, S5_KW), lambda j: (j, 0, 0, 0)),
            pl.BlockSpec((None, 2, 2, S5_SW), lambda j: (j, 0, 0, 0)),
        ],
        compiler_params=_cparams(("arbitrary",)),
        name="s5_ops",
    )(lane, bt, ct)


def _s5_kernel(uc_ref, ul_ref, d_ref, min_ref, mintra_ref, mout_ref, dec_ref, zc_ref, zl_ref,
               ucb_ref, xh_ref, y_ref, *, nc_ctx, nc_lat, n_rb):
    nch = nc_ctx + nc_lat
    t = S5_T
    ctx_len = nc_ctx * t
    bounds = [0] + [nc_ctx + (k + 1) * (nc_lat // n_rb) for k in range(n_rb)]

    nsl = 2 * S5_SW // LANES
    ncs = nsl // 2

    def row_pieces(dd, lo, hi):
        if dd == 0:
            return [(lo, hi, lo)]
        out = []
        if lo < nc_ctx:
            out.append((lo, min(hi, nc_ctx), lo + nc_lat))
        if hi > nc_ctx:
            out.append((max(lo, nc_ctx), hi, max(lo, nc_ctx) - nc_ctx))
        return out

    for rb in range(n_rb):
        lo, hi = bounds[rb], bounds[rb + 1]
        for tt in range(t):
            cols = slice(tt * LANES, (tt + 1) * LANES)
            if lo < nc_ctx:
                ucb_ref[lo:nc_ctx, cols] = uc_ref[pl.ds(lo * t + tt, nc_ctx - lo, stride=t), :].astype(BF16)
            l0 = max(lo, nc_ctx)
            ucb_ref[l0:hi, cols] = ul_ref[pl.ds((l0 - nc_ctx) * t + tt, hi - l0, stride=t), :].astype(BF16)
        for dd in range(2):
            x = jnp.dot(ucb_ref[lo:hi, :], min_ref[dd], preferred_element_type=F32)
            for s0, s1, d0 in row_pieces(dd, lo, hi):
                for k in range(nsl):
                    xh_ref[dd, k, d0:d0 + s1 - s0, :] = x[s0 - lo:s1 - lo, k * LANES:(k + 1) * LANES]

    seg = nch // 8

    def start_row(dd, i):
        return i if dd == 0 else seg - 1 - i

    def dec_slab(dd, k, ri):
        return dec_ref[dd, ri:ri + 1, k * LANES:(k + 1) * LANES]

    dec_b = [[(jnp.broadcast_to(dec_slab(dd, k, 0), (8, LANES)), jnp.broadcast_to(dec_slab(dd, k, 1), (8, LANES)))
              for k in range(ncs)] for dd in range(2)]

    def local_scan(i, st):
        new = []
        for dd in range(2):
            rows = pl.ds(start_row(dd, i), 8, stride=seg)
            for k in range(ncs):
                lr, li = st[2 * (dd * ncs + k)], st[2 * (dd * ncs + k) + 1]
                re_ref, im_ref = xh_ref.at[dd, k], xh_ref.at[dd, ncs + k]
                xr, xi = re_ref[rows, :], im_ref[rows, :]
                re_ref[rows, :] = lr
                im_ref[rows, :] = li
                ar, ai = dec_b[dd][k]
                new += [ar * lr - ai * li + xr, ar * li + ai * lr + xi]
        return tuple(new)

    zero = jnp.zeros((8, LANES), F32)
    ends = lax.fori_loop(0, seg, local_scan, (zero,) * (4 * ncs), unroll=2)

    def cmul(a, b):
        return a[0] * b[0] - a[1] * b[1], a[0] * b[1] + a[1] * b[0]

    def cpow(a, e):
        acc, sq = None, a
        while e:
            if e & 1:
                acc = sq if acc is None else cmul(acc, sq)
            sq = cmul(sq, sq)
            e >>= 1
        return acc

    carries = []
    for dd in range(2):
        order = list(range(8)) if dd == 0 else list(range(7, -1, -1))
        for k in range(ncs):
            a_seg = cpow((dec_slab(dd, k, 0), dec_slab(dd, k, 1)), seg)
            lr, li = ends[2 * (dd * ncs + k)], ends[2 * (dd * ncs + k) + 1]
            c = (jnp.zeros((1, LANES), F32), jnp.zeros((1, LANES), F32))
            by_sublane = {}
            for m in order:
                by_sublane[m] = c
                pr, pi = cmul(a_seg, c)
                c = (pr + lr[m:m + 1, :], pi + li[m:m + 1, :])
            carries.append(jnp.concatenate([by_sublane[m][0] for m in range(8)], axis=0))
            carries.append(jnp.concatenate([by_sublane[m][1] for m in range(8)], axis=0))

    def add_carry(i, gs):
        new = []
        for dd in range(2):
            rows = pl.ds(start_row(dd, i), 8, stride=seg)
            for k in range(ncs):
                gr, gi = gs[2 * (dd * ncs + k)], gs[2 * (dd * ncs + k) + 1]
                re_ref, im_ref = xh_ref.at[dd, k], xh_ref.at[dd, ncs + k]
                re_ref[rows, :] = re_ref[rows, :] + gr
                im_ref[rows, :] = im_ref[rows, :] + gi
                ar, ai = dec_b[dd][k]
                new += [ar * gr - ai * gi, ar * gi + ai * gr]
        return tuple(new)

    lax.fori_loop(0, seg, add_carry, tuple(carries), unroll=2)

    def state_rows(dd, lo, hi):
        parts = [jnp.concatenate([xh_ref[dd, k, d0:d0 + s1 - s0, :] for k in range(nsl)], axis=-1)
                 for s0, s1, d0 in row_pieces(dd, lo, hi)]
        return (parts[0] if len(parts) == 1 else jnp.concatenate(parts, axis=0)).astype(BF16)

    dv = d_ref[...]
    for rb in range(n_rb):
        lo, hi = bounds[rb], bounds[rb + 1]
        y = jnp.dot(ucb_ref[lo:hi, :], mintra_ref[...], preferred_element_type=F32)
        y = y + jnp.dot(state_rows(0, lo, hi), mout_ref[0], preferred_element_type=F32)
        y = y + jnp.dot(state_rows(1, lo, hi), mout_ref[1], preferred_element_type=F32)
        for tt in range(t):
            y_ref[pl.ds(lo * t + tt, hi - lo, stride=t), :] = y[:, tt * LANES:(tt + 1) * LANES]
        if lo < nc_ctx:
            zc_ref[lo * t:ctx_len, :] = jax.nn.gelu(
                y_ref[lo * t:ctx_len, :] + uc_ref[lo * t:ctx_len, :] * dv).astype(zc_ref.dtype)
        p0, p1 = (max(lo, nc_ctx) - nc_ctx) * t, (hi - nc_ctx) * t
        zl_ref[p0:p1, :] = jax.nn.gelu(
            y_ref[ctx_len + p0:ctx_len + p1, :] + ul_ref[p0:p1, :] * dv).astype(zl_ref.dtype)


def _s5_core(u_ctx, u_lat, d_skip, ops, layer):
    m_in, m_intra, m_out, decay = ops
    b, l, w = u_lat.shape
    ctx_len = u_ctx.shape[1]
    n_blk = w // LANES
    nc_ctx, nc_lat = ctx_len // S5_T, l // S5_T
    nch = nc_ctx + nc_lat
    n_rb = 4
    assert nc_ctx % 16 == 0 and nc_lat % (16 * n_rb) == 0 and nch % 8 == 0
    kern = functools.partial(_s5_kernel, nc_ctx=nc_ctx, nc_lat=nc_lat, n_rb=n_rb)
    base = layer * n_blk
    return pl.pallas_call(
        kern,
        out_shape=[jax.ShapeDtypeStruct(u_ctx.shape, BF16), jax.ShapeDtypeStruct(u_lat.shape, BF16)],
        grid=(n_blk, b),
        in_specs=[
            pl.BlockSpec((None, ctx_len, LANES), lambda j, bb: (bb, 0, j)),
            pl.BlockSpec((None, l, LANES), lambda j, bb: (bb, 0, j)),
            pl.BlockSpec((None, 1, LANES), lambda j, bb: (layer, 0, j)),
            pl.BlockSpec((None, 2, S5_KW, 2 * S5_SW), lambda j, bb: (base + j, 0, 0, 0)),
            pl.BlockSpec((None, S5_KW, S5_KW), lambda j, bb: (base + j, 0, 0)),
            pl.BlockSpec((None, 2, 2 * S5_SW, S5_KW), lambda j, bb: (base + j, 0, 0, 0)),
            pl.BlockSpec((None, 2, 2, S5_SW), lambda j, bb: (base + j, 0, 0, 0)),
        ],
        out_specs=[
            pl.BlockSpec((None, ctx_len, LANES), lambda j, bb: (bb, 0, j)),
            pl.BlockSpec((None, l, LANES), lambda j, bb: (bb, 0, j)),
        ],
        scratch_shapes=[
            pltpu.VMEM((nch, S5_KW), BF16),
            pltpu.VMEM((2, 2 * S5_SW // LANES, nch, LANES), F32),
            pltpu.VMEM((nch * S5_T, LANES), F32),
        ],
        compiler_params=_cparams(("arbitrary", "arbitrary")),
        name="s5_core",
    )(u_ctx, u_lat, d_skip.reshape(d_skip.shape[0], 1, w), m_in, m_intra, m_out, decay)


def kernel(x, c, ctx, c_ctx, ada_w, ada_b, norm1_g, norm2_g, ffn_w1, ffn_w3, ffn_w2, gm_w_in, gm_ln_g, gm_ln_b, gm_w_s, gm_b_s, gm_w_out, s5_w_in, s5_a_re, s5_a_im, s5_log_dt, s5_b_re, s5_b_im, s5_c_re, s5_c_im, s5_d, s5_w_glu, final_g):
    bsz, seq, d = x.shape
    depth = ada_w.shape[0]
    ffn_hidden = ffn_w1.shape[-1]
    gm_width = gm_w_out.shape[1]
    s5_layers = [i for i in range(depth) if i % N_MIXERS == 1]
    last_s5 = s5_layers[-1] if s5_layers else -1

    cond_rows = jnp.concatenate([c, c_ctx[None, :]], axis=0)
    mod = _adaln_all(cond_rows, ada_w, ada_b)

    def mods(i, stream):
        if stream == "lat":
            m = mod[i, :bsz]
        else:
            m = mod[i, bsz:bsz + 1]
        return [m[:, None, k * d:(k + 1) * d] for k in range(6)]

    w1, w3, w2 = ffn_w1.astype(BF16), ffn_w3.astype(BF16), ffn_w2.astype(BF16)
    gw_in, gw_out = gm_w_in.astype(BF16), gm_w_out.astype(BF16)
    sw_in, sw_glu = s5_w_in.astype(BF16), s5_w_glu.astype(BF16)
    s5_ops = _s5_ops(s5_a_re, s5_a_im, s5_log_dt, s5_b_re, s5_b_im, s5_c_re, s5_c_im) if s5_layers else None

    h = _embed(x)
    ctx_len = ctx.shape[1]
    hc = ctx.reshape(1, bsz * ctx_len, d)

    for i in range(depth):
        ctx_read = i <= last_s5
        ctx_carry = i < last_s5
        j = i // N_MIXERS
        streams = [("lat", h)]
        if ctx_read:
            streams.append(("ctx", hc))
        new = {}
        if i % N_MIXERS == 0:
            for name, hs in streams:
                if name == "ctx" and not ctx_carry:
                    continue
                sh1, sc1, g1, _, _, _ = mods(i, name)
                u, v = _normmod_matmul(hs, norm1_g[i], sh1, sc1, [(gw_in, j, 0), (gw_in, j, gm_width)],
                                       gm_width, "gelu_pair", [BF16, BF16], 512, "gm_in")
                tg = _gmlp_gate(u, v, gm_ln_g[j], gm_ln_b[j], gm_w_s[j], gm_b_s[j])
                new[name] = _matmul_res(tg, gw_out, j, hs, g1, False, "gm_out")
        else:
            us = {}
            for name, hs in streams:
                sh1, sc1, _, _, _, _ = mods(i, name)
                us[name] = _normmod_matmul(hs, norm1_g[i], sh1, sc1, [(sw_in, j, 0)],
                                           d, "plain", [F32], 512, "s5_in")[0]
            z_ctx, z_lat = _s5_core(us["ctx"].reshape(bsz, ctx_len, d), us["lat"], s5_d, s5_ops, j)
            z_ctx = z_ctx.reshape(1, bsz * ctx_len, d)
            new["lat"] = _matmul_res(z_lat, sw_glu, j, h, mods(i, "lat")[2], True, "s5_out")
            if ctx_carry:
                new["ctx"] = _matmul_res(z_ctx, sw_glu, j, hc, mods(i, "ctx")[2], True, "s5_out")
        h = new["lat"]
        if ctx_carry:
            hc = new["ctx"]

        streams = [("lat", h)] + ([("ctx", hc)] if ctx_carry else [])
        for name, hs in streams:
            _, _, _, sh2, sc2, g2 = mods(i, name)
            tf = _normmod_matmul(hs, norm2_g[i], sh2, sc2, [(w1, i, 0), (w3, i, 0)],
                                 ffn_hidden, "swiglu", [BF16], 512, "ffn_in")[0]
            out = _matmul_res(tf, w2, i, hs, g2, False, "ffn_out")
            if name == "lat":
                h = out
            else:
                hc = out

    return _final_norm(h, final_g)
```

```python
import functools
import math

import jax
import jax.numpy as jnp
from jax import lax
from jax.experimental import pallas as pl
from jax.experimental.pallas import tpu as pltpu

F32 = jnp.float32
BF16 = jnp.bfloat16

EPS = 1e-6
GRID_W = 64
CHUNK = 128
GM_HEADS = 16
S5_GROUP = 16
S5_STATE = 64
N_MIXERS = 2

LANES = 128
S5_T = 8
S5_GB = LANES // S5_GROUP
S5_KW = S5_T * LANES
S5_SW = S5_GB * S5_STATE
VMEM_LIMIT = 56 * 1024 * 1024


def _cparams(sem):
    return pltpu.CompilerParams(dimension_semantics=sem, vmem_limit_bytes=VMEM_LIMIT)


def _adaln_kernel(cond_ref, w_ref, b_ref, o_ref, s_ref, *, n_rows, kc):
    x = cond_ref[...]
    s_ref[...] = x * jax.nn.sigmoid(x)
    d, tn = w_ref.shape
    nslab = tn // LANES

    def body(kk, accs):
        k0 = pl.multiple_of(kk * kc, kc)
        new = []
        for r in range(n_rows):
            sb = s_ref[r, pl.ds(k0, kc), :]
            for n in range(nslab):
                w = w_ref[pl.ds(k0, kc), n * LANES:(n + 1) * LANES]
                new.append(accs[r * nslab + n] + (w * sb).reshape(kc // 8, 8, LANES).sum(axis=0))
        return tuple(new)

    init = tuple(jnp.zeros((8, LANES), F32) for _ in range(n_rows * nslab))
    accs = lax.fori_loop(0, d // kc, body, init)
    o_ref[...] = jnp.zeros(o_ref.shape, F32)
    for r in range(n_rows):
        for n in range(nslab):
            o_ref[r:r + 1, n * LANES:(n + 1) * LANES] = (
                jnp.sum(accs[r * nslab + n], axis=0, keepdims=True) + b_ref[:, n * LANES:(n + 1) * LANES])


def _adaln_all(cond_rows, ada_w, ada_b):
    n_rows, d = cond_rows.shape
    depth, _, n6 = ada_w.shape
    tn = 1536
    cond_b =jnp.broadcast_to(cond_rows[:, :, None], (n_rows, d, LANES))
    return pl.pallas_call(
        functools.partial(_adaln_kernel, n_rows=n_rows, kc=64),
        out_shape=jax.ShapeDtypeStruct((depth, 8, n6), F32),
        grid=(depth, n6 // tn),
        in_specs=[
            pl.BlockSpec((n_rows, d, LANES), lambda l, j: (0, 0, 0)),
            pl.BlockSpec((None, d, tn), lambda l, j: (l, 0, j)),
            pl.BlockSpec((None, 1, tn), lambda l, j: (l, 0, j)),
        ],
        out_specs=pl.BlockSpec((None, 8, tn), lambda l, j: (l, 0, j)),
        scratch_shapes=[pltpu.VMEM((n_rows, d, LANES), F32)],
        compiler_params=_cparams(("arbitrary", "arbitrary")),
        name="adaln",
    )(cond_b, ada_w, ada_b.reshape(depth, 1, n6))


def _embed_kernel(x_ref, o_ref, col_ref, *, rows_per_tile):
    d = x_ref.shape[-1]
    q = d // 4
    k = lax.broadcasted_iota(jnp.int32, (1, q), 1).astype(F32)
    omega = jnp.exp(k * (-math.log(10000.0) / q))
    i = pl.program_id(0)

    @pl.when((i == 0) & (pl.program_id(1) == 0))
    def _():
        c = lax.broadcasted_iota(jnp.int32, (GRID_W, 1), 0).astype(F32)
        ang = c * omega
        col_ref[:, :q] = jnp.sin(ang)
        col_ref[:, q:] = jnp.cos(ang)

    r = (i * rows_per_tile + lax.broadcasted_iota(jnp.int32, (rows_per_tile, 1), 0)).astype(F32)
    ang_r = r * omega
    row_pe = jnp.concatenate([jnp.sin(ang_r), jnp.cos(ang_r)], axis=-1)
    for rr in range(rows_per_tile):
        sl = slice(rr * GRID_W, (rr + 1) * GRID_W)
        o_ref[sl, :2 * q] = x_ref[sl, :2 * q] + row_pe[rr:rr + 1, :]
        o_ref[sl, 2 * q:] = x_ref[sl, 2 * q:] + col_ref[...]


def _embed(x):
    b, l, d = x.shape
    rows_per_tile = 8
    tl = rows_per_tile * GRID_W
    return pl.pallas_call(
        functools.partial(_embed_kernel, rows_per_tile=rows_per_tile),
        out_shape=jax.ShapeDtypeStruct(x.shape, F32),
        grid=(l // tl, b),
        in_specs=[pl.BlockSpec((None, tl, d), lambda i, bb: (bb, i, 0))],
        out_specs=pl.BlockSpec((None, tl, d), lambda i, bb: (bb, i, 0)),
        scratch_shapes=[pltpu.VMEM((GRID_W, d // 2), F32)],
        compiler_params=_cparams(("arbitrary", "arbitrary")),
        name="embed",
    )(x)


def _rmsnorm_kernel(h_ref, g_ref, o_ref):
    x = h_ref[...]
    o_ref[...] = x * lax.rsqrt(jnp.mean(x * x, axis=-1, keepdims=True) + EPS) * g_ref[...]


def _final_norm(h, g):
    b, l, d = h.shape
    tm = 512
    return pl.pallas_call(
        _rmsnorm_kernel,
        out_shape=jax.ShapeDtypeStruct(h.shape, F32),
        grid=(b, l // tm),
        in_specs=[pl.BlockSpec((None, tm, d), lambda bb, i: (bb, i, 0)),
                  pl.BlockSpec((1, d), lambda bb, i: (0, 0))],
        out_specs=pl.BlockSpec((None, tm, d), lambda bb, i: (bb, i, 0)),
        compiler_params=_cparams(("arbitrary", "arbitrary")),
        name="final_norm",
    )(h, g.reshape(1, d))


def _w_index(bb, i, j, *, layer, off):
    return (layer, 0, j + off)


def _normmod_matmul_kernel(hf_ref, hp_ref, g_ref, sh_ref, sc_ref, shn_ref, scn_ref, *rest, n_w, mode, n_pieces):
    w_refs = rest[:n_w]
    out_refs = rest[n_w:-2]
    slots = rest[-2:]
    n = pl.program_id(0) * pl.num_programs(1) + pl.program_id(1)
    j = pl.program_id(2)
    g = g_ref[...]

    def normmod(x, sh, sc):
        r = lax.rsqrt(jnp.mean(x * x, axis=-1, keepdims=True) + EPS)
        return (x * r * (g * (1.0 + sc)) + sh).astype(BF16)

    @pl.when((n == 0) & (j == 0))
    def _():
        slots[0][...] = normmod(hf_ref[...], sh_ref[...], sc_ref[...])

    rp = hp_ref.shape[0]
    piece = jnp.minimum(j, n_pieces - 1)

    def step(cur_ref, nxt_ref):
        a = cur_ref[...]
        accs = [jnp.dot(a, w[...], preferred_element_type=F32) for w in w_refs]
        if mode == "plain":
            out_refs[0][...] = accs[0].astype(out_refs[0].dtype)
        elif mode == "gelu_pair":
            out_refs[0][...] = jax.nn.gelu(accs[0].astype(BF16)).astype(out_refs[0].dtype)
            out_refs[1][...] = jax.nn.gelu(accs[1].astype(BF16)).astype(out_refs[1].dtype)
        elif mode == "swiglu":
            out_refs[0][...] = (jax.nn.silu(accs[0]) * accs[1]).astype(out_refs[0].dtype)
        else:
            raise ValueError(mode)
        nxt_ref[pl.ds(pl.multiple_of(piece * rp, rp), rp), :] = normmod(hp_ref[...], shn_ref[...], scn_ref[...])

    pl.when(n % 2 == 0)(lambda: step(slots[0], slots[1]))
    pl.when(n % 2 == 1)(lambda: step(slots[1], slots[0]))


def _normmod_matmul(h, g, shift, scale, weights, n, mode, out_dtypes, tn, name):
    b, t, d = h.shape
    tm = min(1024, t)
    n_i, n_j = t // tm, n // tn
    n_pieces = min(n_j, 8)
    rp = tm // n_pieces
    assert rp * n_pieces == tm and rp % 16 == 0

    def next_tile(bb, i):
        nn = jnp.minimum(bb * n_i + i + 1, b * n_i - 1)
        return nn // n_i, nn % n_i

    def piece_index(bb, i, j):
        nb, ni = next_tile(bb, i)
        return (nb, ni * n_pieces + jnp.minimum(j, n_pieces - 1), 0)

    def next_mod_index(bb, i, j):
        return (next_tile(bb, i)[0], 0, 0)

    in_specs = [
        pl.BlockSpec((None, tm, d), lambda bb, i, j: (0, 0, 0)),
        pl.BlockSpec((None, rp, d), piece_index),
        pl.BlockSpec((1, d), lambda bb, i, j: (0, 0)),
        pl.BlockSpec((None, 1, d), lambda bb, i, j: (0, 0, 0)),
        pl.BlockSpec((None, 1, d), lambda bb, i, j: (0, 0, 0)),
        pl.BlockSpec((None, 1, d), next_mod_index),
        pl.BlockSpec((None, 1, d), next_mod_index),
    ]
    for _, layer, off in weights:
        in_specs.append(pl.BlockSpec((None, d, tn), functools.partial(_w_index, layer=layer, off=off // tn)))
    out_shape = [jax.ShapeDtypeStruct((b, t, n), dt) for dt in out_dtypes]
    out_specs = [pl.BlockSpec((None, tm, tn), lambda bb, i, j: (bb, i, j)) for _ in out_dtypes]
    return pl.pallas_call(
        functools.partial(_normmod_matmul_kernel, n_w=len(weights), mode=mode, n_pieces=n_pieces),
        out_shape=out_shape,
        grid=(b, n_i, n_j),
        in_specs=in_specs,
        out_specs=out_specs,
        scratch_shapes=[pltpu.VMEM((tm, d), BF16), pltpu.VMEM((tm, d), BF16)],
        compiler_params=_cparams(("arbitrary", "arbitrary", "arbitrary")),
        name=name,
    )(h, h, g.reshape(1, d), shift, scale, shift, scale, *[w for w, _, _ in weights])


def _matmul_res_kernel(a_ref, h_ref, gate_ref, *rest, glu):
    a = a_ref[...]
    acc = jnp.dot(a, rest[0][...], preferred_element_type=F32)
    if glu:
        gl = jnp.dot(a, rest[1][...], preferred_element_type=F32)
        acc = acc * jax.nn.sigmoid(gl)
    o_ref = rest[-1]
    o_ref[...] = h_ref[...] + gate_ref[...] * acc


def _matmul_res(a, w, layer, h, gate, glu, name):
    b, t, k = a.shape
    n = h.shape[-1]
    tm = min(1024, t)
    tn = 512
    in_specs = [
        pl.BlockSpec((None, tm, k), lambda bb, i, j: (bb, i, 0)),
        pl.BlockSpec((None, tm, tn), lambda bb, i, j: (bb, i, j)),
        pl.BlockSpec((None, 1, tn), lambda bb, i, j: (bb, 0, j)),
        pl.BlockSpec((None, k, tn), functools.partial(_w_index, layer=layer, off=0)),
    ]
    args = [a, h, gate, w]
    if glu:
        in_specs.append(pl.BlockSpec((None, k, tn), functools.partial(_w_index, layer=layer, off=n // tn)))
        args.append(w)
    return pl.pallas_call(
        functools.partial(_matmul_res_kernel, glu=glu),
        out_shape=jax.ShapeDtypeStruct(h.shape, F32),
        grid=(b, t // tm, n // tn),
        in_specs=in_specs,
        out_specs=pl.BlockSpec((None, tm, tn), lambda bb, i, j: (bb, i, j)),
        compiler_params=_cparams(("arbitrary", "arbitrary", "arbitrary")),
        name=name,
    )(*args)


def _gmlp_gate_kernel(u_ref, v_ref, lng_ref, lnb_ref, ws_ref, bs_ref, t_ref):
    v = v_ref[...].astype(F32)
    mu = jnp.mean(v, axis=-1, keepdims=True)
    xc = v - mu
    var = jnp.mean(xc * xc, axis=-1, keepdims=True)
    vn = (xc * lax.rsqrt(var + EPS) * lng_ref[...] + lnb_ref[...]).astype(BF16)
    hd = v.shape[-1] // GM_HEADS
    for cc in range(v.shape[0] // CHUNK):
        rows = slice(cc * CHUNK, (cc + 1) * CHUNK)
        for hh in range(GM_HEADS):
            sl = slice(hh * hd, (hh + 1) * hd)
            s = jnp.dot(ws_ref[hh], vn[rows, sl], preferred_element_type=F32) + bs_ref[hh]
            t_ref[rows, sl] = (u_ref[rows, sl].astype(F32) * s).astype(BF16)


def _gmlp_gate(u, v, ln_g, ln_b, w_s, b_s):
    b, t, e = u.shape
    tr = 2 * CHUNK
    return pl.pallas_call(
        _gmlp_gate_kernel,
        out_shape=jax.ShapeDtypeStruct((b, t, e), BF16),
        grid=(b, t // tr),
        in_specs=[
            pl.BlockSpec((None, tr, e), lambda bb, i: (bb, i, 0)),
            pl.BlockSpec((None, tr, e), lambda bb, i: (bb, i, 0)),
            pl.BlockSpec((1, e), lambda bb, i: (0, 0)),
            pl.BlockSpec((1, e), lambda bb, i: (0, 0)),
            pl.BlockSpec((GM_HEADS, CHUNK, CHUNK), lambda bb, i: (0, 0, 0)),
            pl.BlockSpec((GM_HEADS, CHUNK, 1), lambda bb, i: (0, 0, 0)),
        ],
        out_specs=pl.BlockSpec((None, tr, e), lambda bb, i: (bb, i, 0)),
        compiler_params=_cparams(("arbitrary", "arbitrary")),
        name="gmlp_gate",
    )(u, v, ln_g.reshape(1, e), ln_b.reshape(1, e), w_s.astype(BF16), b_s.reshape(GM_HEADS, CHUNK, 1))


def _dot_split(a, b):
    a_hi, b_hi = a.astype(BF16), b.astype(BF16)
    a_lo = (a - a_hi.astype(F32)).astype(BF16)
    b_lo = (b - b_hi.astype(F32)).astype(BF16)
    dot = functools.partial(jnp.dot, preferred_element_type=F32)
    return dot(a_hi, b_hi) + (dot(a_hi, b_lo) + dot(a_lo, b_hi))


def _s5_ops_kernel(lane_ref, bt_ref, ct_ref, min_ref, mintra_ref, mout_ref, dec_ref):
    t = S5_T
    sw = S5_SW
    mask_b =(lax.broadcasted_iota(jnp.int32, (LANES, sw), 0) // S5_GROUP
              == lax.broadcasted_iota(jnp.int32, (LANES, sw), 1) // S5_STATE)
    mask_c = (lax.broadcasted_iota(jnp.int32, (sw, LANES), 0) // S5_STATE
              == lax.broadcasted_iota(jnp.int32, (sw, LANES), 1) // S5_GROUP)

    def to_rows(v):
        return jnp.concatenate(
            [jnp.broadcast_to(v[:, i * LANES:(i + 1) * LANES], (LANES, LANES)).T for i in range(sw // LANES)],
            axis=0)

    lag = []
    for dd in range(2):
        a_re, a_im = lane_ref[dd, 0:1, :], lane_ref[dd, 1:2, :]
        dt = jnp.exp(lane_ref[dd, 2:3, :])
        mag = jnp.exp(a_re * dt)
        p1_re, p1_im = mag * jnp.cos(a_im * dt), mag * jnp.sin(a_im * dt)
        nr, ni = p1_re - 1.0, p1_im
        den = a_re * a_re + a_im * a_im
        cf_re, cf_im = (nr * a_re + ni * a_im) / den, (ni * a_re - nr * a_im) / den
        b_re = jnp.where(mask_b, bt_ref[dd, 0], 0.0)
        b_im = jnp.where(mask_b, bt_ref[dd, 1], 0.0)
        bb_re = cf_re * b_re - cf_im * b_im
        bb_im = cf_re * b_im + cf_im * b_re
        pw = [(jnp.ones_like(p1_re), jnp.zeros_like(p1_re))]
        for _ in range(t):
            pr, pi = pw[-1]
            pw.append((pr * p1_re - pi * p1_im, pr * p1_im + pi * p1_re))
        dec_ref[dd, 0:1, :] = pw[t][0]
        dec_ref[dd, 1:2, :] = pw[t][1]
        ab = [(pr * bb_re - pi * bb_im, pr * bb_im + pi * bb_re) for pr, pi in pw[:t]]
        for s in range(t):
            m_re, m_im = ab[t - 1 - s] if dd == 0 else ab[s]
            min_ref[dd, s * LANES:(s + 1) * LANES, 0:sw] = m_re.astype(BF16)
            min_ref[dd, s * LANES:(s + 1) * LANES, sw:2 * sw] = m_im.astype(BF16)
        r1_re, r1_im = to_rows(p1_re), to_rows(p1_im)
        qr = jnp.where(mask_c, ct_ref[dd, 0], 0.0)
        qi = jnp.where(mask_c, ct_ref[dd, 1], 0.0)
        lag.append(_dot_split(jnp.concatenate([x[0] for x in ab], axis=0), qr)
                   - _dot_split(jnp.concatenate([x[1] for x in ab], axis=0), qi))
        mo = []
        for _ in range(t + 1):
            mo.append((qr, -qi))
            qr, qi = qr * r1_re - qi * r1_im, qr * r1_im + qi * r1_re
        for tt in range(t):
            m_re, m_im = mo[tt + 1] if dd == 0 else mo[t - tt]
            mout_ref[dd, 0:sw, tt * LANES:(tt + 1) * LANES] = m_re.astype(BF16)
            mout_ref[dd, sw:2 * sw, tt * LANES:(tt + 1) * LANES] = m_im.astype(BF16)

    def lag_blk(dd, k):
        return lag[dd][k * LANES:(k + 1) * LANES, :]

    for s in range(t):
        for tt in range(t):
            if s < tt:
                blk = lag_blk(0, tt - s)
            elif s > tt:
                blk = lag_blk(1, s - tt)
            else:
                blk = lag_blk(0, 0) + lag_blk(1, 0)
            mintra_ref[s * LANES:(s + 1) * LANES, tt * LANES:(tt + 1) * LANES] = blk.astype(BF16)


def _s5_ops(a_re, a_im, log_dt, b_re, b_im, c_re, c_im):
    nl, _, n_g, p = a_re.shape
    q = b_re.shape[-1]
    n_blk = n_g // S5_GB
    nb = nl * n_blk
    lane = jnp.stack([a_re, a_im, jnp.broadcast_to(log_dt[..., None], a_re.shape)], axis=2)
    lane = jnp.transpose(lane.reshape(nl, 2, 3, n_blk, S5_SW), (0, 3, 1, 2, 4)).reshape(nb, 2, 3, S5_SW)
    bt = jnp.stack([b_re, b_im], axis=2)
    bt = jnp.swapaxes(bt, -1, -2).reshape(nl, 2, 2, n_blk, S5_GB * q, p)
    bt = jnp.tile(jnp.transpose(bt, (0, 3, 1, 2, 4, 5)), (1, 1, 1, 1, 1, S5_GB)).reshape(nb, 2, 2, LANES, S5_SW)
    ct = jnp.stack([c_re, c_im], axis=2)
    ct = jnp.swapaxes(ct, -1, -2).reshape(nl, 2, 2, n_blk, S5_SW, q)
    ct = jnp.tile(jnp.transpose(ct, (0, 3, 1, 2, 4, 5)), (1, 1, 1, 1, 1, S5_GB)).reshape(nb, 2, 2, S5_SW, LANES)
    return pl.pallas_call(
        _s5_ops_kernel,
        out_shape=[
            jax.ShapeDtypeStruct((nb, 2, S5_KW, 2 * S5_SW), BF16),
            jax.ShapeDtypeStruct((nb, S5_KW, S5_KW), BF16),
            jax.ShapeDtypeStruct((nb, 2, 2 * S5_SW, S5_KW), BF16),
            jax.ShapeDtypeStruct((nb, 2, 2, S5_SW), F32),
        ],
        grid=(nb,),
        in_specs=[
            pl.BlockSpec((None, 2, 3, S5_SW), lambda j: (j, 0, 0, 0)),
            pl.BlockSpec((None, 2, 2, LANES, S5_SW), lambda j: (j, 0, 0, 0, 0)),
            pl.BlockSpec((None, 2, 2, S5_SW, LANES), lambda j: (j, 0, 0, 0, 0)),
        ],
        out_specs=[
            pl.BlockSpec((None, 2, S5_KW, 2 * S5_SW), lambda j: (j, 0, 0, 0)),
            pl.BlockSpec((None, S5_KW, S5_KW), lambda j: (j, 0, 0)),
            pl.BlockSpec((None, 2, 2 * S5_SW, S5_KW), lambda j: (j, 0, 0, 0)),
            pl.BlockSpec((None, 2, 2, S5_SW), lambda j: (j, 0, 0, 0)),
        ],
        compiler_params=_cparams(("arbitrary",)),
        name="s5_ops",
    )(lane, bt, ct)


def _s5_kernel(uc_ref, ul_ref, d_ref, min_ref, mintra_ref, mout_ref, dec_ref, zc_ref, zl_ref,
               ucb_ref, xh_ref, y_ref, *, nc_ctx, nc_lat, n_rb):
    nch = nc_ctx + nc_lat
    t = S5_T
    ctx_len = nc_ctx * t
    bounds = [0] + [nc_ctx + (k + 1) * (nc_lat // n_rb) for k in range(n_rb)]

    nsl = 2 * S5_SW // LANES
    ncs = nsl // 2

    def row_pieces(dd, lo, hi):
        if dd == 0:
            return [(lo, hi, lo)]
        out = []
        if lo < nc_ctx:
            out.append((lo, min(hi, nc_ctx), lo + nc_lat))
        if hi > nc_ctx:
            out.append((max(lo, nc_ctx), hi, max(lo, nc_ctx) - nc_ctx))
        return out

    for rb in range(n_rb):
        lo, hi = bounds[rb], bounds[rb + 1]
        for tt in range(t):
            cols = slice(tt * LANES, (tt + 1) * LANES)
            if lo < nc_ctx:
                ucb_ref[lo:nc_ctx, cols] = uc_ref[pl.ds(lo * t + tt, nc_ctx - lo, stride=t), :].astype(BF16)
            l0 = max(lo, nc_ctx)
            ucb_ref[l0:hi, cols] = ul_ref[pl.ds((l0 - nc_ctx) * t + tt, hi - l0, stride=t), :].astype(BF16)
        for dd in range(2):
            x = jnp.dot(ucb_ref[lo:hi, :], min_ref[dd], preferred_element_type=F32)
            for s0, s1, d0 in row_pieces(dd, lo, hi):
                for k in range(nsl):
                    xh_ref[dd, k, d0:d0 + s1 - s0, :] = x[s0 - lo:s1 - lo, k * LANES:(k + 1) * LANES]

    seg = nch // 8

    def start_row(dd, i):
        return i if dd == 0 else seg - 1 - i

    def dec_slab(dd, k, ri):
        return dec_ref[dd, ri:ri + 1, k * LANES:(k + 1) * LANES]

    dec_b = [[(jnp.broadcast_to(dec_slab(dd, k, 0), (8, LANES)), jnp.broadcast_to(dec_slab(dd, k, 1), (8, LANES)))
              for k in range(ncs)] for dd in range(2)]

    def local_scan(i, st):
        new = []
        for dd in range(2):
            rows = pl.ds(start_row(dd, i), 8, stride=seg)
            for k in range(ncs):
                lr, li = st[2 * (dd * ncs + k)], st[2 * (dd * ncs + k) + 1]
                re_ref, im_ref = xh_ref.at[dd, k], xh_ref.at[dd, ncs + k]
                xr, xi = re_ref[rows, :], im_ref[rows, :]
                re_ref[rows, :] = lr
                im_ref[rows, :] = li
                ar, ai = dec_b[dd][k]
                new += [ar * lr - ai * li + xr, ar * li + ai * lr + xi]
        return tuple(new)

    zero = jnp.zeros((8, LANES), F32)
    ends = lax.fori_loop(0, seg, local_scan, (zero,) * (4 * ncs), unroll=2)

    def cmul(a, b):
        return a[0] * b[0] - a[1] * b[1], a[0] * b[1] + a[1] * b[0]

    def cpow(a, e):
        acc, sq = None, a
        while e:
            if e & 1:
                acc = sq if acc is None else cmul(acc, sq)
            sq = cmul(sq, sq)
            e >>= 1
        return acc

    carries = []
    for dd in range(2):
        order = list(range(8)) if dd == 0 else list(range(7, -1, -1))
        for k in range(ncs):
            a_seg = cpow((dec_slab(dd, k, 0), dec_slab(dd, k, 1)), seg)
            lr, li = ends[2 * (dd * ncs + k)], ends[2 * (dd * ncs + k) + 1]
            c = (jnp.zeros((1, LANES), F32), jnp.zeros((1, LANES), F32))
            by_sublane = {}
            for m in order:
                by_sublane[m] = c
                pr, pi = cmul(a_seg, c)
                c = (pr + lr[m:m + 1, :], pi + li[m:m + 1, :])
            carries.append(jnp.concatenate([by_sublane[m][0] for m in range(8)], axis=0))
            carries.append(jnp.concatenate([by_sublane[m][1] for m in range(8)], axis=0))

    def add_carry(i, gs):
        new = []
        for dd in range(2):
            rows = pl.ds(start_row(dd, i), 8, stride=seg)
            for k in range(ncs):
                gr, gi = gs[2 * (dd * ncs + k)], gs[2 * (dd * ncs + k) + 1]
                re_ref, im_ref = xh_ref.at[dd, k], xh_ref.at[dd, ncs + k]
                re_ref[rows, :] = re_ref[rows, :] + gr
                im_ref[rows, :] = im_ref[rows, :] + gi
                ar, ai = dec_b[dd][k]
                new += [ar * gr - ai * gi, ar * gi + ai * gr]
        return tuple(new)

    lax.fori_loop(0, seg, add_carry, tuple(carries), unroll=2)

    def state_rows(dd, lo, hi):
        parts = [jnp.concatenate([xh_ref[dd, k, d0:d0 + s1 - s0, :] for k in range(nsl)], axis=-1)
                 for s0, s1, d0 in row_pieces(dd, lo, hi)]
        return (parts[0] if len(parts) == 1 else jnp.concatenate(parts, axis=0)).astype(BF16)

    dv = d_ref[...]
    for rb in range(n_rb):
        lo, hi = bounds[rb], bounds[rb + 1]
        y = jnp.dot(ucb_ref[lo:hi, :], mintra_ref[...], preferred_element_type=F32)
        y = y + jnp.dot(state_rows(0, lo, hi), mout_ref[0], preferred_element_type=F32)
        y = y + jnp.dot(state_rows(1, lo, hi), mout_ref[1], preferred_element_type=F32)
        for tt in range(t):
            y_ref[pl.ds(lo * t + tt, hi - lo, stride=t), :] = y[:, tt * LANES:(tt + 1) * LANES]
        if lo < nc_ctx:
            zc_ref[lo * t:ctx_len, :] = jax.nn.gelu(
                (y_ref[lo * t:ctx_len, :] + uc_ref[lo * t:ctx_len, :] * dv).astype(BF16))
        p0, p1 = (max(lo, nc_ctx) - nc_ctx) * t, (hi - nc_ctx) * t
        zl_ref[p0:p1, :] = jax.nn.gelu(
            (y_ref[ctx_len + p0:ctx_len + p1, :] + ul_ref[p0:p1, :] * dv).astype(BF16))


def _s5_core(u_ctx, u_lat, d_skip, ops, layer):
    m_in, m_intra, m_out, decay = ops
    b, l, w = u_lat.shape
    ctx_len = u_ctx.shape[1]
    n_blk = w // LANES
    nc_ctx, nc_lat = ctx_len // S5_T, l // S5_T
    nch = nc_ctx + nc_lat
    n_rb = 4
    assert nc_ctx % 16 == 0 and nc_lat % (16 * n_rb) == 0 and nch % 8 == 0
    kern = functools.partial(_s5_kernel, nc_ctx=nc_ctx, nc_lat=nc_lat, n_rb=n_rb)
    base = layer * n_blk
    return pl.pallas_call(
        kern,
        out_shape=[jax.ShapeDtypeStruct(u_ctx.shape, BF16), jax.ShapeDtypeStruct(u_lat.shape, BF16)],
        grid=(n_blk, b),
        in_specs=[
            pl.BlockSpec((None, ctx_len, LANES), lambda j, bb: (bb, 0, j)),
            pl.BlockSpec((None, l, LANES), lambda j, bb: (bb, 0, j)),
            pl.BlockSpec((None, 1, LANES), lambda j, bb: (layer, 0, j)),
            pl.BlockSpec((None, 2, S5_KW, 2 * S5_SW), lambda j, bb: (base + j, 0, 0, 0)),
            pl.BlockSpec((None, S5_KW, S5_KW), lambda j, bb: (base + j, 0, 0)),
            pl.BlockSpec((None, 2, 2 * S5_SW, S5_KW), lambda j, bb: (base + j, 0, 0, 0)),
            pl.BlockSpec((None, 2, 2, S5_SW), lambda j, bb: (base + j, 0, 0, 0)),
        ],
        out_specs=[
            pl.BlockSpec((None, ctx_len, LANES), lambda j, bb: (bb, 0, j)),
            pl.BlockSpec((None, l, LANES), lambda j, bb: (bb, 0, j)),
        ],
        scratch_shapes=[
            pltpu.VMEM((nch, S5_KW), BF16),
            pltpu.VMEM((2, 2 * S5_SW // LANES, nch, LANES), F32),
            pltpu.VMEM((nch * S5_T, LANES), F32),
        ],
        compiler_params=_cparams(("arbitrary", "arbitrary")),
        name="s5_core",
    )(u_ctx, u_lat, d_skip.reshape(d_skip.shape[0], 1, w), m_in, m_intra, m_out, decay)


def kernel(x, c, ctx, c_ctx, ada_w, ada_b, norm1_g, norm2_g, ffn_w1, ffn_w3, ffn_w2, gm_w_in, gm_ln_g, gm_ln_b, gm_w_s, gm_b_s, gm_w_out, s5_w_in, s5_a_re, s5_a_im, s5_log_dt, s5_b_re, s5_b_im, s5_c_re, s5_c_im, s5_d, s5_w_glu, final_g):
    bsz, seq, d = x.shape
    depth = ada_w.shape[0]
    ffn_hidden = ffn_w1.shape[-1]
    gm_width = gm_w_out.shape[1]
    s5_layers = [i for i in range(depth) if i % N_MIXERS == 1]
    last_s5 = s5_layers[-1] if s5_layers else -1

    cond_rows = jnp.concatenate([c, c_ctx[None, :]], axis=0)
    mod = _adaln_all(cond_rows, ada_w, ada_b)

    def mods(i, stream):
        if stream == "lat":
            m = mod[i, :bsz]
        else:
            m = mod[i, bsz:bsz + 1]
        return [m[:, None, k * d:(k + 1) * d] for k in range(6)]

    w1, w3, w2 = ffn_w1.astype(BF16), ffn_w3.astype(BF16), ffn_w2.astype(BF16)
    gw_in, gw_out = gm_w_in.astype(BF16), gm_w_out.astype(BF16)
    sw_in, sw_glu = s5_w_in.astype(BF16), s5_w_glu.astype(BF16)
    s5_ops = _s5_ops(s5_a_re, s5_a_im, s5_log_dt, s5_b_re, s5_b_im, s5_c_re, s5_c_im) if s5_layers else None

    h = _embed(x)
    ctx_len = ctx.shape[1]
    hc = ctx.reshape(1, bsz * ctx_len, d)

    for i in range(depth):
        ctx_read = i <= last_s5
        ctx_carry = i < last_s5
        j = i // N_MIXERS
        streams = [("lat", h)]
        if ctx_read:
            streams.append(("ctx", hc))
        new = {}
        if i % N_MIXERS == 0:
            for name, hs in streams:
                if name == "ctx" and not ctx_carry:
                    continue
                sh1, sc1, g1, _, _, _ = mods(i, name)
                u, v = _normmod_matmul(hs, norm1_g[i], sh1, sc1, [(gw_in, j, 0), (gw_in, j, gm_width)],
                                       gm_width, "gelu_pair", [BF16, BF16], 512, "gm_in")
                tg = _gmlp_gate(u, v, gm_ln_g[j], gm_ln_b[j], gm_w_s[j], gm_b_s[j])
                new[name] = _matmul_res(tg, gw_out, j, hs, g1, False, "gm_out")
        else:
            us = {}
            for name, hs in streams:
                sh1, sc1, _, _, _, _ = mods(i, name)
                us[name] = _normmod_matmul(hs, norm1_g[i], sh1, sc1, [(sw_in, j, 0)],
                                           d, "plain", [F32], 512, "s5_in")[0]
            z_ctx, z_lat = _s5_core(us["ctx"].reshape(bsz, ctx_len, d), us["lat"], s5_d, s5_ops, j)
            z_ctx = z_ctx.reshape(1, bsz * ctx_len, d)
            new["lat"] = _matmul_res(z_lat, sw_glu, j, h, mods(i, "lat")[2], True, "s5_out")
            if ctx_carry:
                new["ctx"] = _matmul_res(z_ctx, sw_glu, j, hc, mods(i, "ctx")[2], True, "s5_out")
        h = new["lat"]
        if ctx_carry:
            hc = new["ctx"]

        streams = [("lat", h)] + ([("ctx", hc)] if ctx_carry else [])
        for name, hs in streams:
            _, _, _, sh2, sc2, g2 = mods(i, name)
            tf = _normmod_matmul(hs, norm2_g[i], sh2, sc2, [(w1, i, 0), (w3, i, 0)],
                                 ffn_hidden, "swiglu", [BF16], 512, "ffn_in")[0]
            out = _matmul_res(tf, w2, i, hs, g2, False, "ffn_out")
            if name == "lat":
                h = out
            else:
                hc = out

    return _final_norm(h, final_g)
```

```python
import functools
import math

import jax
import jax.numpy as jnp
from jax import lax
from jax.experimental import pallas as pl
from jax.experimental.pallas import tpu as pltpu

F32 = jnp.float32
BF16 = jnp.bfloat16

EPS = 1e-6
GRID_W = 64
CHUNK = 128
GM_HEADS = 16
S5_GROUP = 16
S5_STATE = 64
N_MIXERS = 2

LANES = 128
S5_T = 8
S5_GB = LANES // S5_GROUP
S5_KW = S5_T * LANES
S5_SW = S5_GB * S5_STATE
VMEM_LIMIT = 56 * 1024 * 1024


def _cparams(sem):
    return pltpu.CompilerParams(dimension_semantics=sem, vmem_limit_bytes=VMEM_LIMIT)


def _adaln_kernel(cond_ref, w_ref, b_ref, o_ref, s_ref, *, n_rows, kc):
    x = cond_ref[...]
    s_ref[...] = x * jax.nn.sigmoid(x)
    d, tn = w_ref.shape
    nslab = tn // LANES

    def body(kk, accs):
        k0 = pl.multiple_of(kk * kc, kc)
        new = []
        for r in range(n_rows):
            sb = s_ref[r, pl.ds(k0, kc), :]
            for n in range(nslab):
                w = w_ref[pl.ds(k0, kc), n * LANES:(n + 1) * LANES]
                new.append(accs[r * nslab + n] + (w * sb).reshape(kc // 8, 8, LANES).sum(axis=0))
        return tuple(new)

    init = tuple(jnp.zeros((8, LANES), F32) for _ in range(n_rows * nslab))
    accs = lax.fori_loop(0, d // kc, body, init)
    o_ref[...] = jnp.zeros(o_ref.shape, F32)
    for r in range(n_rows):
        for n in range(nslab):
            o_ref[r:r + 1, n * LANES:(n + 1) * LANES] = (
                jnp.sum(accs[r * nslab + n], axis=0, keepdims=True) + b_ref[:, n * LANES:(n + 1) * LANES])


def _adaln_all(cond_rows, ada_w, ada_b):
    n_rows, d = cond_rows.shape
    depth, _, n6 = ada_w.shape
    tn = 1536
    cond_b =jnp.broadcast_to(cond_rows[:, :, None], (n_rows, d, LANES))
    return pl.pallas_call(
        functools.partial(_adaln_kernel, n_rows=n_rows, kc=64),
        out_shape=jax.ShapeDtypeStruct((depth, 8, n6), F32),
        grid=(depth, n6 // tn),
        in_specs=[
            pl.BlockSpec((n_rows, d, LANES), lambda l, j: (0, 0, 0)),
            pl.BlockSpec((None, d, tn), lambda l, j: (l, 0, j)),
            pl.BlockSpec((None, 1, tn), lambda l, j: (l, 0, j)),
        ],
        out_specs=pl.BlockSpec((None, 8, tn), lambda l, j: (l, 0, j)),
        scratch_shapes=[pltpu.VMEM((n_rows, d, LANES), F32)],
        compiler_params=_cparams(("arbitrary", "arbitrary")),
        name="adaln",
    )(cond_b, ada_w, ada_b.reshape(depth, 1, n6))


def _embed_kernel(x_ref, o_ref, col_ref, *, rows_per_tile):
    d = x_ref.shape[-1]
    q = d // 4
    k = lax.broadcasted_iota(jnp.int32, (1, q), 1).astype(F32)
    omega = jnp.exp(k * (-math.log(10000.0) / q))
    i = pl.program_id(0)

    @pl.when((i == 0) & (pl.program_id(1) == 0))
    def _():
        c = lax.broadcasted_iota(jnp.int32, (GRID_W, 1), 0).astype(F32)
        ang = c * omega
        col_ref[:, :q] = jnp.sin(ang)
        col_ref[:, q:] = jnp.cos(ang)

    r = (i * rows_per_tile + lax.broadcasted_iota(jnp.int32, (rows_per_tile, 1), 0)).astype(F32)
    ang_r = r * omega
    row_pe = jnp.concatenate([jnp.sin(ang_r), jnp.cos(ang_r)], axis=-1)
    for rr in range(rows_per_tile):
        sl = slice(rr * GRID_W, (rr + 1) * GRID_W)
        o_ref[sl, :2 * q] = x_ref[sl, :2 * q] + row_pe[rr:rr + 1, :]
        o_ref[sl, 2 * q:] = x_ref[sl, 2 * q:] + col_ref[...]


def _embed(x):
    b, l, d = x.shape
    rows_per_tile = 8
    tl = rows_per_tile * GRID_W
    return pl.pallas_call(
        functools.partial(_embed_kernel, rows_per_tile=rows_per_tile),
        out_shape=jax.ShapeDtypeStruct(x.shape, F32),
        grid=(l // tl, b),
        in_specs=[pl.BlockSpec((None, tl, d), lambda i, bb: (bb, i, 0))],
        out_specs=pl.BlockSpec((None, tl, d), lambda i, bb: (bb, i, 0)),
        scratch_shapes=[pltpu.VMEM((GRID_W, d // 2), F32)],
        compiler_params=_cparams(("arbitrary", "arbitrary")),
        name="embed",
    )(x)


def _rmsnorm_kernel(h_ref, g_ref, o_ref):
    x = h_ref[...]
    o_ref[...] = x * lax.rsqrt(jnp.mean(x * x, axis=-1, keepdims=True) + EPS) * g_ref[...]


def _final_norm(h, g):
    b, l, d = h.shape
    tm = 512
    return pl.pallas_call(
        _rmsnorm_kernel,
        out_shape=jax.ShapeDtypeStruct(h.shape, F32),
        grid=(b, l // tm),
        in_specs=[pl.BlockSpec((None, tm, d), lambda bb, i: (bb, i, 0)),
                  pl.BlockSpec((1, d), lambda bb, i: (0, 0))],
        out_specs=pl.BlockSpec((None, tm, d), lambda bb, i: (bb, i, 0)),
        compiler_params=_cparams(("arbitrary", "arbitrary")),
        name="final_norm",
    )(h, g.reshape(1, d))


def _w_index(bb, i, j, *, layer, off):
    return (layer, 0, j + off)


def _normmod_matmul_kernel(hf_ref, hp_ref, g_ref, sh_ref, sc_ref, shn_ref, scn_ref, *rest, n_w, mode, n_pieces):
    w_refs = rest[:n_w]
    out_refs = rest[n_w:-2]
    slots = rest[-2:]
    n = pl.program_id(0) * pl.num_programs(1) + pl.program_id(1)
    j = pl.program_id(2)
    g = g_ref[...]

    def normmod(x, sh, sc):
        r = lax.rsqrt(jnp.mean(x * x, axis=-1, keepdims=True) + EPS)
        return (x * r * (g * (1.0 + sc)) + sh).astype(BF16)

    @pl.when((n == 0) & (j == 0))
    def _():
        slots[0][...] = normmod(hf_ref[...], sh_ref[...], sc_ref[...])

    rp = hp_ref.shape[0]
    piece = jnp.minimum(j, n_pieces - 1)

    def step(cur_ref, nxt_ref):
        a = cur_ref[...]
        accs = [jnp.dot(a, w[...], preferred_element_type=F32) for w in w_refs]
        if mode == "plain":
            out_refs[0][...] = accs[0].astype(out_refs[0].dtype)
        elif mode == "gelu_pair":
            out_refs[0][...] = jax.nn.gelu(accs[0].astype(BF16)).astype(out_refs[0].dtype)
            out_refs[1][...] = jax.nn.gelu(accs[1].astype(BF16)).astype(out_refs[1].dtype)
        elif mode == "swiglu":
            out_refs[0][...] = (jax.nn.silu(accs[0]) * accs[1]).astype(out_refs[0].dtype)
        else:
            raise ValueError(mode)
        nxt_ref[pl.ds(pl.multiple_of(piece * rp, rp), rp), :] = normmod(hp_ref[...], shn_ref[...], scn_ref[...])

    pl.when(n % 2 == 0)(lambda: step(slots[0], slots[1]))
    pl.when(n % 2 == 1)(lambda: step(slots[1], slots[0]))


def _normmod_matmul(h, g, shift, scale, weights, n, mode, out_dtypes, tn, name):
    b, t, d = h.shape
    tm = min(1024, t)
    n_i, n_j = t // tm, n // tn
    n_pieces = min(n_j, 8)
    rp = tm // n_pieces
    assert rp * n_pieces == tm and rp % 16 == 0

    def next_tile(bb, i):
        nn = jnp.minimum(bb * n_i + i + 1, b * n_i - 1)
        return nn // n_i, nn % n_i

    def piece_index(bb, i, j):
        nb, ni = next_tile(bb, i)
        return (nb, ni * n_pieces + jnp.minimum(j, n_pieces - 1), 0)

    def next_mod_index(bb, i, j):
        return (next_tile(bb, i)[0], 0, 0)

    in_specs = [
        pl.BlockSpec((None, tm, d), lambda bb, i, j: (0, 0, 0)),
        pl.BlockSpec((None, rp, d), piece_index),
        pl.BlockSpec((1, d), lambda bb, i, j: (0, 0)),
        pl.BlockSpec((None, 1, d), lambda bb, i, j: (0, 0, 0)),
        pl.BlockSpec((None, 1, d), lambda bb, i, j: (0, 0, 0)),
        pl.BlockSpec((None, 1, d), next_mod_index),
        pl.BlockSpec((None, 1, d), next_mod_index),
    ]
    for _, layer, off in weights:
        in_specs.append(pl.BlockSpec((None, d, tn), functools.partial(_w_index, layer=layer, off=off // tn)))
    out_shape = [jax.ShapeDtypeStruct((b, t, n), dt) for dt in out_dtypes]
    out_specs = [pl.BlockSpec((None, tm, tn), lambda bb, i, j: (bb, i, j)) for _ in out_dtypes]
    return pl.pallas_call(
        functools.partial(_normmod_matmul_kernel, n_w=len(weights), mode=mode, n_pieces=n_pieces),
        out_shape=out_shape,
        grid=(b, n_i, n_j),
        in_specs=in_specs,
        out_specs=out_specs,
        scratch_shapes=[pltpu.VMEM((tm, d), BF16), pltpu.VMEM((tm, d), BF16)],
        compiler_params=_cparams(("arbitrary", "arbitrary", "arbitrary")),
        name=name,
    )(h, h, g.reshape(1, d), shift, scale, shift, scale, *[w for w, _, _ in weights])


def _matmul_res_kernel(a_ref, h_ref, gate_ref, *rest, glu):
    a = a_ref[...]
    acc = jnp.dot(a, rest[0][...], preferred_element_type=F32)
    if glu:
        gl = jnp.dot(a, rest[1][...], preferred_element_type=F32)
        acc = acc * jax.nn.sigmoid(gl)
    o_ref = rest[-1]
    o_ref[...] = h_ref[...] + gate_ref[...] * acc


def _matmul_res(a, w, layer, h, gate, glu, name):
    b, t, k = a.shape
    n = h.shape[-1]
    tm = min(1024, t)
    tn = 512
    in_specs = [
        pl.BlockSpec((None, tm, k), lambda bb, i, j: (bb, i, 0)),
        pl.BlockSpec((None, tm, tn), lambda bb, i, j: (bb, i, j)),
        pl.BlockSpec((None, 1, tn), lambda bb, i, j: (bb, 0, j)),
        pl.BlockSpec((None, k, tn), functools.partial(_w_index, layer=layer, off=0)),
    ]
    args = [a, h, gate, w]
    if glu:
        in_specs.append(pl.BlockSpec((None, k, tn), functools.partial(_w_index, layer=layer, off=n // tn)))
        args.append(w)
    return pl.pallas_call(
        functools.partial(_matmul_res_kernel, glu=glu),
        out_shape=jax.ShapeDtypeStruct(h.shape, F32),
        grid=(b, t // tm, n // tn),
        in_specs=in_specs,
        out_specs=pl.BlockSpec((None, tm, tn), lambda bb, i, j: (bb, i, j)),
        compiler_params=_cparams(("arbitrary", "arbitrary", "arbitrary")),
        name=name,
    )(*args)


def _gmlp_gate_kernel(u_ref, v_ref, lng_ref, lnb_ref, ws_ref, bs_ref, t_ref):
    v = v_ref[...].astype(F32)
    mu = jnp.mean(v, axis=-1, keepdims=True)
    xc = v - mu
    var = jnp.mean(xc * xc, axis=-1, keepdims=True)
    vn = (xc * lax.rsqrt(var + EPS) * lng_ref[...] + lnb_ref[...]).astype(BF16)
    hd = v.shape[-1] // GM_HEADS
    for cc in range(v.shape[0] // CHUNK):
        rows = slice(cc * CHUNK, (cc + 1) * CHUNK)
        for hh in range(GM_HEADS):
            sl = slice(hh * hd, (hh + 1) * hd)
            s = jnp.dot(ws_ref[hh], vn[rows, sl], preferred_element_type=F32) + bs_ref[hh]
            t_ref[rows, sl] = (u_ref[rows, sl].astype(F32) * s).astype(BF16)


def _gmlp_gate(u, v, ln_g, ln_b, w_s, b_s):
    b, t, e = u.shape
    tr = 2 * CHUNK
    return pl.pallas_call(
        _gmlp_gate_kernel,
        out_shape=jax.ShapeDtypeStruct((b, t, e), BF16),
        grid=(b, t // tr),
        in_specs=[
            pl.BlockSpec((None, tr, e), lambda bb, i: (bb, i, 0)),
            pl.BlockSpec((None, tr, e), lambda bb, i: (bb, i, 0)),
            pl.BlockSpec((1, e), lambda bb, i: (0, 0)),
            pl.BlockSpec((1, e), lambda bb, i: (0, 0)),
            pl.BlockSpec((GM_HEADS, CHUNK, CHUNK), lambda bb, i: (0, 0, 0)),
            pl.BlockSpec((GM_HEADS, CHUNK, 1), lambda bb, i: (0, 0, 0)),
        ],
        out_specs=pl.BlockSpec((None, tr, e), lambda bb, i: (bb, i, 0)),
        compiler_params=_cparams(("arbitrary", "arbitrary")),
        name="gmlp_gate",
    )(u, v, ln_g.reshape(1, e), ln_b.reshape(1, e), w_s.astype(BF16), b_s.reshape(GM_HEADS, CHUNK, 1))


def _dot_split(a, b):
    a_hi, b_hi = a.astype(BF16), b.astype(BF16)
    a_lo = (a - a_hi.astype(F32)).astype(BF16)
    b_lo = (b - b_hi.astype(F32)).astype(BF16)
    dot = functools.partial(jnp.dot, preferred_element_type=F32)
    return dot(a_hi, b_hi) + (dot(a_hi, b_lo) + dot(a_lo, b_hi))


def _s5_ops_kernel(lane_ref, bt_ref, ct_ref, min_ref, mintra_ref, mout_ref, dec_ref):
    t = S5_T
    sw = S5_SW
    mask_b =(lax.broadcasted_iota(jnp.int32, (LANES, sw), 0) // S5_GROUP
              == lax.broadcasted_iota(jnp.int32, (LANES, sw), 1) // S5_STATE)
    mask_c = (lax.broadcasted_iota(jnp.int32, (sw, LANES), 0) // S5_STATE
              == lax.broadcasted_iota(jnp.int32, (sw, LANES), 1) // S5_GROUP)

    def to_rows(v):
        return jnp.concatenate(
            [jnp.broadcast_to(v[:, i * LANES:(i + 1) * LANES], (LANES, LANES)).T for i in range(sw // LANES)],
            axis=0)

    lag = []
    for dd in range(2):
        a_re, a_im = lane_ref[dd, 0:1, :], lane_ref[dd, 1:2, :]
        dt = jnp.exp(lane_ref[dd, 2:3, :])
        mag = jnp.exp(a_re * dt)
        p1_re, p1_im = mag * jnp.cos(a_im * dt), mag * jnp.sin(a_im * dt)
        nr, ni = p1_re - 1.0, p1_im
        den = a_re * a_re + a_im * a_im
        cf_re, cf_im = (nr * a_re + ni * a_im) / den, (ni * a_re - nr * a_im) / den
        b_re = jnp.where(mask_b, bt_ref[dd, 0], 0.0)
        b_im = jnp.where(mask_b, bt_ref[dd, 1], 0.0)
        bb_re = cf_re * b_re - cf_im * b_im
        bb_im = cf_re * b_im + cf_im * b_re
        pw = [(jnp.ones_like(p1_re), jnp.zeros_like(p1_re))]
        for _ in range(t):
            pr, pi = pw[-1]
            pw.append((pr * p1_re - pi * p1_im, pr * p1_im + pi * p1_re))
        dec_ref[dd, 0:1, :] = pw[t][0]
        dec_ref[dd, 1:2, :] = pw[t][1]
        ab = [(pr * bb_re - pi * bb_im, pr * bb_im + pi * bb_re) for pr, pi in pw[:t]]
        for s in range(t):
            m_re, m_im = ab[t - 1 - s] if dd == 0 else ab[s]
            min_ref[dd, s * LANES:(s + 1) * LANES, 0:sw] = m_re.astype(BF16)
            min_ref[dd, s * LANES:(s + 1) * LANES, sw:2 * sw] = m_im.astype(BF16)
        r1_re, r1_im = to_rows(p1_re), to_rows(p1_im)
        qr = jnp.where(mask_c, ct_ref[dd, 0], 0.0)
        qi = jnp.where(mask_c, ct_ref[dd, 1], 0.0)
        lag.append(_dot_split(jnp.concatenate([x[0] for x in ab], axis=0), qr)
                   - _dot_split(jnp.concatenate([x[1] for x in ab], axis=0), qi))
        mo = []
        for _ in range(t + 1):
            mo.append((qr, -qi))
            qr, qi = qr * r1_re - qi * r1_im, qr * r1_im + qi * r1_re
        for tt in range(t):
            m_re, m_im = mo[tt + 1] if dd == 0 else mo[t - tt]
            mout_ref[dd, 0:sw, tt * LANES:(tt + 1) * LANES] = m_re.astype(BF16)
            mout_ref[dd, sw:2 * sw, tt * LANES:(tt + 1) * LANES] = m_im.astype(BF16)

    def lag_blk(dd, k):
        return lag[dd][k * LANES:(k + 1) * LANES, :]

    for s in range(t):
        for tt in range(t):
            if s < tt:
                blk = lag_blk(0, tt - s)
            elif s > tt:
                blk = lag_blk(1, s - tt)
            else:
                blk = lag_blk(0, 0) + lag_blk(1, 0)
            mintra_ref[s * LANES:(s + 1) * LANES, tt * LANES:(tt + 1) * LANES] = blk.astype(BF16)


def _s5_ops(a_re, a_im, log_dt, b_re, b_im, c_re, c_im):
    nl, _, n_g, p = a_re.shape
    q = b_re.shape[-1]
    n_blk = n_g // S5_GB
    nb = nl * n_blk
    lane = jnp.stack([a_re, a_im, jnp.broadcast_to(log_dt[..., None], a_re.shape)], axis=2)
    lane = jnp.transpose(lane.reshape(nl, 2, 3, n_blk, S5_SW), (0, 3, 1, 2, 4)).reshape(nb, 2, 3, S5_SW)
    bt = jnp.stack([b_re, b_im], axis=2)
    bt = jnp.swapaxes(bt, -1, -2).reshape(nl, 2, 2, n_blk, S5_GB * q, p)
    bt = jnp.tile(jnp.transpose(bt, (0, 3, 1, 2, 4, 5)), (1, 1, 1, 1, 1, S5_GB)).reshape(nb, 2, 2, LANES, S5_SW)
    ct = jnp.stack([c_re, c_im], axis=2)
    ct = jnp.swapaxes(ct, -1, -2).reshape(nl, 2, 2, n_blk, S5_SW, q)
    ct = jnp.tile(jnp.transpose(ct, (0, 3, 1, 2, 4, 5)), (1, 1, 1, 1, 1, S5_GB)).reshape(nb, 2, 2, S5_SW, LANES)
    return pl.pallas_call(
        _s5_ops_kernel,
        out_shape=[
            jax.ShapeDtypeStruct((nb, 2, S5_KW, 2 * S5_SW), BF16),
            jax.ShapeDtypeStruct((nb, S5_KW, S5_KW), BF16),
            jax.ShapeDtypeStruct((nb, 2, 2 * S5_SW, S5_KW), BF16),
            jax.ShapeDtypeStruct((nb, 2, 2, S5_SW), F32),
        ],
        grid=(nb,),
        in_specs=[
            pl.BlockSpec((None, 2, 3, S5_SW), lambda j: (j, 0, 0, 0)),
            pl.BlockSpec((None, 2, 2, LANES, S5_SW), lambda j: (j, 0, 0, 0, 0)),
            pl.BlockSpec((None, 2, 2, S5_SW, LANES), lambda j: (j, 0, 0, 0, 0)),
        ],
        out_specs=[
            pl.BlockSpec((None, 2, S5_KW, 2 * S5_SW), lambda j: (j, 0, 0, 0)),
            pl.BlockSpec((None, S5_KW, S5_KW), lambda j: (j, 0, 0)),
            pl.BlockSpec((None, 2, 2 * S5_SW, S5_KW), lambda j: (j, 0, 0, 0)),
            pl.BlockSpec((None, 2, 2, S5_SW), lambda j: (j, 0, 0, 0)),
        ],
        compiler_params=_cparams(("arbitrary",)),
        name="s5_ops",
    )(lane, bt, ct)


def _s5_kernel(uc_ref, ul_ref, d_ref, min_ref, mintra_ref, mout_ref, dec_ref, zc_ref, zl_ref,
               ucb_ref, xh_ref, y_ref, *, nc_ctx, nc_lat, n_rb):
    nch = nc_ctx + nc_lat
    t = S5_T
    ctx_len = nc_ctx * t
    bounds = [0] + [nc_ctx + (k + 1) * (nc_lat // n_rb) for k in range(n_rb)]

    nsl = 2 * S5_SW // LANES
    ncs = nsl // 2

    def row_pieces(dd, lo, hi):
        if dd == 0:
            return [(lo, hi, lo)]
        out = []
        if lo < nc_ctx:
            out.append((lo, min(hi, nc_ctx), lo + nc_lat))
        if hi > nc_ctx:
            out.append((max(lo, nc_ctx), hi, max(lo, nc_ctx) - nc_ctx))
        return out

    for rb in range(n_rb):
        lo, hi = bounds[rb], bounds[rb + 1]
        for tt in range(t):
            cols = slice(tt * LANES, (tt + 1) * LANES)
            if lo < nc_ctx:
                ucb_ref[lo:nc_ctx, cols] = uc_ref[pl.ds(lo * t + tt, nc_ctx - lo, stride=t), :].astype(BF16)
            l0 = max(lo, nc_ctx)
            ucb_ref[l0:hi, cols] = ul_ref[pl.ds((l0 - nc_ctx) * t + tt, hi - l0, stride=t), :].astype(BF16)
        for dd in range(2):
            x = jnp.dot(ucb_ref[lo:hi, :], min_ref[dd], preferred_element_type=F32)
            for s0, s1, d0 in row_pieces(dd, lo, hi):
                for k in range(nsl):
                    xh_ref[dd, k, d0:d0 + s1 - s0, :] = x[s0 - lo:s1 - lo, k * LANES:(k + 1) * LANES]

    seg = nch // 8
    assert seg % 4 == 0

    def start_row(dd, i):
        return i if dd == 0 else seg - 1 - i

    def dec_slab(dd, k, ri):
        return dec_ref[dd, ri:ri + 1, k * LANES:(k + 1) * LANES]

    dec_b = [[(jnp.broadcast_to(dec_slab(dd, k, 0), (8, LANES)), jnp.broadcast_to(dec_slab(dd, k, 1), (8, LANES)))
              for k in range(ncs)] for dd in range(2)]

    def cmul(a, b):
        return a[0] * b[0] - a[1] * b[1], a[0] * b[1] + a[1] * b[0]

    def cmuladd(a, h, x):
        return a[0] * h[0] - a[1] * h[1] + x[0], a[0] * h[1] + a[1] * h[0] + x[1]

    dec4_b = [[cmul(cmul(a, a), cmul(a, a)) for a in per_dir] for per_dir in dec_b]

    def load_rows(dd, k, i):
        rows = pl.ds(start_row(dd, i), 8, stride=seg)
        return xh_ref.at[dd, k][rows, :], xh_ref.at[dd, ncs + k][rows, :]

    def store_rows(dd, k, i, h):
        rows = pl.ds(start_row(dd, i), 8, stride=seg)
        xh_ref.at[dd, k][rows, :] = h[0]
        xh_ref.at[dd, ncs + k][rows, :] = h[1]

    def segment_ends(i4, st):
        new = []
        for dd in range(2):
            for k in range(ncs):
                a = dec_b[dd][k]
                q = load_rows(dd, k, 4 * i4)
                for m in range(1, 4):
                    q = cmuladd(a, q, load_rows(dd, k, 4 * i4 + m))
                new += list(cmuladd(dec4_b[dd][k], (st[2 * (dd * ncs + k)], st[2 * (dd * ncs + k) + 1]), q))
        return tuple(new)

    zero = jnp.zeros((8, LANES), F32)
    ends = lax.fori_loop(0, seg // 4, segment_ends, (zero,) * (4 * ncs))

    def cpow(a, e):
        acc, sq = None, a
        while e:
            if e & 1:
                acc = sq if acc is None else cmul(acc, sq)
            sq = cmul(sq, sq)
            e >>= 1
        return acc

    carries = []
    for dd in range(2):
        order = list(range(8)) if dd == 0 else list(range(7, -1, -1))
        for k in range(ncs):
            a_seg = cpow((dec_slab(dd, k, 0), dec_slab(dd, k, 1)), seg)
            lr, li = ends[2 * (dd * ncs + k)], ends[2 * (dd * ncs + k) + 1]
            c = (jnp.zeros((1, LANES), F32), jnp.zeros((1, LANES), F32))
            by_sublane = {}
            for m in order:
                by_sublane[m] = c
                pr, pi = cmul(a_seg, c)
                c = (pr + lr[m:m + 1, :], pi + li[m:m + 1, :])
            carries.append(jnp.concatenate([by_sublane[m][0] for m in range(8)], axis=0))
            carries.append(jnp.concatenate([by_sublane[m][1] for m in range(8)], axis=0))

    def scan_rows(i, st):
        new = []
        for dd in range(2):
            for k in range(ncs):
                s0 = (st[2 * (dd * ncs + k)], st[2 * (dd * ncs + k) + 1])
                x0 = load_rows(dd, k, i)
                store_rows(dd, k, i, s0)
                new += list(cmuladd(dec_b[dd][k], s0, x0))
        return tuple(new)

    lax.fori_loop(0, seg, scan_rows, tuple(carries), unroll=2)

    def state_rows(dd, lo, hi):
        parts = [jnp.concatenate([xh_ref[dd, k, d0:d0 + s1 - s0, :] for k in range(nsl)], axis=-1)
                 for s0, s1, d0 in row_pieces(dd, lo, hi)]
        return (parts[0] if len(parts) == 1 else jnp.concatenate(parts, axis=0)).astype(BF16)

    dv = d_ref[...]
    for rb in range(n_rb):
        lo, hi = bounds[rb], bounds[rb + 1]
        y = jnp.dot(ucb_ref[lo:hi, :], mintra_ref[...], preferred_element_type=F32)
        y = y + jnp.dot(state_rows(0, lo, hi), mout_ref[0], preferred_element_type=F32)
        y = y + jnp.dot(state_rows(1, lo, hi), mout_ref[1], preferred_element_type=F32)
        for tt in range(t):
            y_ref[pl.ds(lo * t + tt, hi - lo, stride=t), :] = y[:, tt * LANES:(tt + 1) * LANES]
        if lo < nc_ctx:
            zc_ref[lo * t:ctx_len, :] = jax.nn.gelu(
                (y_ref[lo * t:ctx_len, :] + uc_ref[lo * t:ctx_len, :] * dv).astype(BF16))
        p0, p1 = (max(lo, nc_ctx) - nc_ctx) * t, (hi - nc_ctx) * t
        zl_ref[p0:p1, :] = jax.nn.gelu(
            (y_ref[ctx_len + p0:ctx_len + p1, :] + ul_ref[p0:p1, :] * dv).astype(BF16))


def _s5_core(u_ctx, u_lat, d_skip, ops, layer):
    m_in, m_intra, m_out, decay = ops
    b, l, w = u_lat.shape
    ctx_len = u_ctx.shape[1]
    n_blk = w // LANES
    nc_ctx, nc_lat = ctx_len // S5_T, l // S5_T
    nch = nc_ctx + nc_lat
    n_rb = 4
    assert nc_ctx % 16 == 0 and nc_lat % (16 * n_rb) == 0 and nch % 8 == 0
    kern = functools.partial(_s5_kernel, nc_ctx=nc_ctx, nc_lat=nc_lat, n_rb=n_rb)
    base = layer * n_blk
    return pl.pallas_call(
        kern,
        out_shape=[jax.ShapeDtypeStruct(u_ctx.shape, BF16), jax.ShapeDtypeStruct(u_lat.shape, BF16)],
        grid=(n_blk, b),
        in_specs=[
            pl.BlockSpec((None, ctx_len, LANES), lambda j, bb: (bb, 0, j)),
            pl.BlockSpec((None, l, LANES), lambda j, bb: (bb, 0, j)),
            pl.BlockSpec((None, 1, LANES), lambda j, bb: (layer, 0, j)),
            pl.BlockSpec((None, 2, S5_KW, 2 * S5_SW), lambda j, bb: (base + j, 0, 0, 0)),
            pl.BlockSpec((None, S5_KW, S5_KW), lambda j, bb: (base + j, 0, 0)),
            pl.BlockSpec((None, 2, 2 * S5_SW, S5_KW), lambda j, bb: (base + j, 0, 0, 0)),
            pl.BlockSpec((None, 2, 2, S5_SW), lambda j, bb: (base + j, 0, 0, 0)),
        ],
        out_specs=[
            pl.BlockSpec((None, ctx_len, LANES), lambda j, bb: (bb, 0, j)),
            pl.BlockSpec((None, l, LANES), lambda j, bb: (bb, 0, j)),
        ],
        scratch_shapes=[
            pltpu.VMEM((nch, S5_KW), BF16),
            pltpu.VMEM((2, 2 * S5_SW // LANES, nch, LANES), F32),
            pltpu.VMEM((nch * S5_T, LANES), F32),
        ],
        compiler_params=_cparams(("arbitrary", "arbitrary")),
        name="s5_core",
    )(u_ctx, u_lat, d_skip.reshape(d_skip.shape[0], 1, w), m_in, m_intra, m_out, decay)


def kernel(x, c, ctx, c_ctx, ada_w, ada_b, norm1_g, norm2_g, ffn_w1, ffn_w3, ffn_w2, gm_w_in, gm_ln_g, gm_ln_b, gm_w_s, gm_b_s, gm_w_out, s5_w_in, s5_a_re, s5_a_im, s5_log_dt, s5_b_re, s5_b_im, s5_c_re, s5_c_im, s5_d, s5_w_glu, final_g):
    bsz, seq, d = x.shape
    depth = ada_w.shape[0]
    ffn_hidden = ffn_w1.shape[-1]
    gm_width = gm_w_out.shape[1]
    s5_layers = [i for i in range(depth) if i % N_MIXERS == 1]
    last_s5 = s5_layers[-1] if s5_layers else -1

    cond_rows = jnp.concatenate([c, c_ctx[None, :]], axis=0)
    mod = _adaln_all(cond_rows, ada_w, ada_b)

    def mods(i, stream):
        if stream == "lat":
            m = mod[i, :bsz]
        else:
            m = mod[i, bsz:bsz + 1]
        return [m[:, None, k * d:(k + 1) * d] for k in range(6)]

    w1, w3, w2 = ffn_w1.astype(BF16), ffn_w3.astype(BF16), ffn_w2.astype(BF16)
    gw_in, gw_out = gm_w_in.astype(BF16), gm_w_out.astype(BF16)
    sw_in, sw_glu = s5_w_in.astype(BF16), s5_w_glu.astype(BF16)
    s5_ops = _s5_ops(s5_a_re, s5_a_im, s5_log_dt, s5_b_re, s5_b_im, s5_c_re, s5_c_im) if s5_layers else None

    h = _embed(x)
    ctx_len = ctx.shape[1]
    hc = ctx.reshape(1, bsz * ctx_len, d)

    for i in range(depth):
        ctx_read = i <= last_s5
        ctx_carry = i < last_s5
        j = i // N_MIXERS
        streams = [("lat", h)]
        if ctx_read:
            streams.append(("ctx", hc))
        new = {}
        if i % N_MIXERS == 0:
            for name, hs in streams:
                if name == "ctx" and not ctx_carry:
                    continue
                sh1, sc1, g1, _, _, _ = mods(i, name)
                u, v = _normmod_matmul(hs, norm1_g[i], sh1, sc1, [(gw_in, j, 0), (gw_in, j, gm_width)],
                                       gm_width, "gelu_pair", [BF16, BF16], 512, "gm_in")
                tg = _gmlp_gate(u, v, gm_ln_g[j], gm_ln_b[j], gm_w_s[j], gm_b_s[j])
                new[name] = _matmul_res(tg, gw_out, j, hs, g1, False, "gm_out")
        else:
            us = {}
            for name, hs in streams:
                sh1, sc1, _, _, _, _ = mods(i, name)
                us[name] = _normmod_matmul(hs, norm1_g[i], sh1, sc1, [(sw_in, j, 0)],
                                           d, "plain", [F32], 512, "s5_in")[0]
            z_ctx, z_lat = _s5_core(us["ctx"].reshape(bsz, ctx_len, d), us["lat"], s5_d, s5_ops, j)
            z_ctx = z_ctx.reshape(1, bsz * ctx_len, d)
            new["lat"] = _matmul_res(z_lat, sw_glu, j, h, mods(i, "lat")[2], True, "s5_out")
            if ctx_carry:
                new["ctx"] = _matmul_res(z_ctx, sw_glu, j, hc, mods(i, "ctx")[2], True, "s5_out")
        h = new["lat"]
        if ctx_carry:
            hc = new["ctx"]

        streams = [("lat", h)] + ([("ctx", hc)] if ctx_carry else [])
        for name, hs in streams:
            _, _, _, sh2, sc2, g2 = mods(i, name)
            tf = _normmod_matmul(hs, norm2_g[i], sh2, sc2, [(w1, i, 0), (w3, i, 0)],
                                 ffn_hidden, "swiglu", [BF16], 512, "ffn_in")[0]
            out = _matmul_res(tf, w2, i, hs, g2, False, "ffn_out")
            if name == "lat":
                h = out
            else:
                hc = out

    return _final_norm(h, final_g)
```

```python
import functools
import math

import jax
import jax.numpy as jnp
from jax import lax
from jax.experimental import pallas as pl
from jax.experimental.pallas import tpu as pltpu

F32 = jnp.float32
BF16 = jnp.bfloat16

EPS = 1e-6
GRID_W = 64
CHUNK = 128
GM_HEADS = 16
S5_GROUP = 16
S5_STATE = 64
N_MIXERS = 2

LANES = 128
S5_T = 8
S5_GB = LANES // S5_GROUP
S5_KW = S5_T * LANES
S5_SW = S5_GB * S5_STATE
VMEM_LIMIT = 56 * 1024 * 1024


def _cparams(sem):
    return pltpu.CompilerParams(dimension_semantics=sem, vmem_limit_bytes=VMEM_LIMIT)


def _adaln_kernel(cond_ref, w_ref, b_ref, o_ref, s_ref, *, n_rows, kc):
    x = cond_ref[...]
    s_ref[...] = x * jax.nn.sigmoid(x)
    d, tn = w_ref.shape
    nslab = tn // LANES

    def body(kk, accs):
        k0 = pl.multiple_of(kk * kc, kc)
        new = []
        for r in range(n_rows):
            sb = s_ref[r, pl.ds(k0, kc), :]
            for n in range(nslab):
                w = w_ref[pl.ds(k0, kc), n * LANES:(n + 1) * LANES]
                new.append(accs[r * nslab + n] + (w * sb).reshape(kc // 8, 8, LANES).sum(axis=0))
        return tuple(new)

    init = tuple(jnp.zeros((8, LANES), F32) for _ in range(n_rows * nslab))
    accs = lax.fori_loop(0, d // kc, body, init)
    o_ref[...] = jnp.zeros(o_ref.shape, F32)
    for r in range(n_rows):
        for n in range(nslab):
            o_ref[r:r + 1, n * LANES:(n + 1) * LANES] = (
                jnp.sum(accs[r * nslab + n], axis=0, keepdims=True) + b_ref[:, n * LANES:(n + 1) * LANES])


def _adaln_all(cond_rows, ada_w, ada_b):
    n_rows, d = cond_rows.shape
    depth, _, n6 = ada_w.shape
    tn = 1536
    cond_b =jnp.broadcast_to(cond_rows[:, :, None], (n_rows, d, LANES))
    return pl.pallas_call(
        functools.partial(_adaln_kernel, n_rows=n_rows, kc=64),
        out_shape=jax.ShapeDtypeStruct((depth, 8, n6), F32),
        grid=(depth, n6 // tn),
        in_specs=[
            pl.BlockSpec((n_rows, d, LANES), lambda l, j: (0, 0, 0)),
            pl.BlockSpec((None, d, tn), lambda l, j: (l, 0, j)),
            pl.BlockSpec((None, 1, tn), lambda l, j: (l, 0, j)),
        ],
        out_specs=pl.BlockSpec((None, 8, tn), lambda l, j: (l, 0, j)),
        scratch_shapes=[pltpu.VMEM((n_rows, d, LANES), F32)],
        compiler_params=_cparams(("arbitrary", "arbitrary")),
        name="adaln",
    )(cond_b, ada_w, ada_b.reshape(depth, 1, n6))


def _embed_kernel(x_ref, o_ref, col_ref, *, rows_per_tile):
    d = x_ref.shape[-1]
    q = d // 4
    k = lax.broadcasted_iota(jnp.int32, (1, q), 1).astype(F32)
    omega = jnp.exp(k * (-math.log(10000.0) / q))
    i = pl.program_id(0)

    @pl.when((i == 0) & (pl.program_id(1) == 0))
    def _():
        c = lax.broadcasted_iota(jnp.int32, (GRID_W, 1), 0).astype(F32)
        ang = c * omega
        col_ref[:, :q] = jnp.sin(ang)
        col_ref[:, q:] = jnp.cos(ang)

    r = (i * rows_per_tile + lax.broadcasted_iota(jnp.int32, (rows_per_tile, 1), 0)).astype(F32)
    ang_r = r * omega
    row_pe = jnp.concatenate([jnp.sin(ang_r), jnp.cos(ang_r)], axis=-1)
    for rr in range(rows_per_tile):
        sl = slice(rr * GRID_W, (rr + 1) * GRID_W)
        o_ref[sl, :2 * q] = x_ref[sl, :2 * q] + row_pe[rr:rr + 1, :]
        o_ref[sl, 2 * q:] = x_ref[sl, 2 * q:] + col_ref[...]


def _embed(x):
    b, l, d = x.shape
    rows_per_tile = 8
    tl = rows_per_tile * GRID_W
    return pl.pallas_call(
        functools.partial(_embed_kernel, rows_per_tile=rows_per_tile),
        out_shape=jax.ShapeDtypeStruct(x.shape, F32),
        grid=(l // tl, b),
        in_specs=[pl.BlockSpec((None, tl, d), lambda i, bb: (bb, i, 0))],
        out_specs=pl.BlockSpec((None, tl, d), lambda i, bb: (bb, i, 0)),
        scratch_shapes=[pltpu.VMEM((GRID_W, d // 2), F32)],
        compiler_params=_cparams(("arbitrary", "arbitrary")),
        name="embed",
    )(x)


def _rmsnorm_kernel(h_ref, g_ref, o_ref):
    x = h_ref[...]
    o_ref[...] = x * lax.rsqrt(jnp.mean(x * x, axis=-1, keepdims=True) + EPS) * g_ref[...]


def _final_norm(h, g):
    b, l, d = h.shape
    tm = 512
    return pl.pallas_call(
        _rmsnorm_kernel,
        out_shape=jax.ShapeDtypeStruct(h.shape, F32),
        grid=(b, l // tm),
        in_specs=[pl.BlockSpec((None, tm, d), lambda bb, i: (bb, i, 0)),
                  pl.BlockSpec((1, d), lambda bb, i: (0, 0))],
        out_specs=pl.BlockSpec((None, tm, d), lambda bb, i: (bb, i, 0)),
        compiler_params=_cparams(("arbitrary", "arbitrary")),
        name="final_norm",
    )(h, g.reshape(1, d))


def _w_index(bb, i, j, *, layer, off):
    return (layer, 0, j + off)


def _normmod_matmul_kernel(hf_ref, hp_ref, g_ref, sh_ref, sc_ref, shn_ref, scn_ref, *rest, n_w, mode, n_pieces):
    w_refs = rest[:n_w]
    out_refs = rest[n_w:-2]
    slots = rest[-2:]
    n = pl.program_id(0) * pl.num_programs(1) + pl.program_id(1)
    j = pl.program_id(2)
    g = g_ref[...]

    def normmod(x, sh, sc):
        r = lax.rsqrt(jnp.mean(x * x, axis=-1, keepdims=True) + EPS)
        return (x * r * (g * (1.0 + sc)) + sh).astype(BF16)

    @pl.when((n == 0) & (j == 0))
    def _():
        slots[0][...] = normmod(hf_ref[...], sh_ref[...], sc_ref[...])

    rp = hp_ref.shape[0]
    piece = jnp.minimum(j, n_pieces - 1)

    def step(cur_ref, nxt_ref):
        a = cur_ref[...]
        accs = [jnp.dot(a, w[...], preferred_element_type=F32) for w in w_refs]
        if mode == "plain":
            out_refs[0][...] = accs[0].astype(out_refs[0].dtype)
        elif mode == "gelu_pair":
            out_refs[0][...] = jax.nn.gelu(accs[0].astype(BF16)).astype(out_refs[0].dtype)
            out_refs[1][...] = jax.nn.gelu(accs[1].astype(BF16)).astype(out_refs[1].dtype)
        elif mode == "swiglu":
            half = 0.5 * accs[0]
            prod = half * accs[1]
            out_refs[0][...] = (prod + prod * jnp.tanh(half)).astype(out_refs[0].dtype)
        else:
            raise ValueError(mode)
        nxt_ref[pl.ds(pl.multiple_of(piece * rp, rp), rp), :] = normmod(hp_ref[...], shn_ref[...], scn_ref[...])

    pl.when(n % 2 == 0)(lambda: step(slots[0], slots[1]))
    pl.when(n % 2 == 1)(lambda: step(slots[1], slots[0]))


def _normmod_matmul(h, g, shift, scale, weights, n, mode, out_dtypes, tn, name):
    b, t, d = h.shape
    tm = min(1024, t)
    n_i, n_j = t // tm, n // tn
    n_pieces = min(n_j, 8)
    rp = tm // n_pieces
    assert rp * n_pieces == tm and rp % 16 == 0

    def next_tile(bb, i):
        nn = jnp.minimum(bb * n_i + i + 1, b * n_i - 1)
        return nn // n_i, nn % n_i

    def piece_index(bb, i, j):
        nb, ni = next_tile(bb, i)
        return (nb, ni * n_pieces + jnp.minimum(j, n_pieces - 1), 0)

    def next_mod_index(bb, i, j):
        return (next_tile(bb, i)[0], 0, 0)

    in_specs = [
        pl.BlockSpec((None, tm, d), lambda bb, i, j: (0, 0, 0)),
        pl.BlockSpec((None, rp, d), piece_index),
        pl.BlockSpec((1, d), lambda bb, i, j: (0, 0)),
        pl.BlockSpec((None, 1, d), lambda bb, i, j: (0, 0, 0)),
        pl.BlockSpec((None, 1, d), lambda bb, i, j: (0, 0, 0)),
        pl.BlockSpec((None, 1, d), next_mod_index),
        pl.BlockSpec((None, 1, d), next_mod_index),
    ]
    for _, layer, off in weights:
        in_specs.append(pl.BlockSpec((None, d, tn), functools.partial(_w_index, layer=layer, off=off // tn)))
    out_shape = [jax.ShapeDtypeStruct((b, t, n), dt) for dt in out_dtypes]
    out_specs = [pl.BlockSpec((None, tm, tn), lambda bb, i, j: (bb, i, j)) for _ in out_dtypes]
    return pl.pallas_call(
        functools.partial(_normmod_matmul_kernel, n_w=len(weights), mode=mode, n_pieces=n_pieces),
        out_shape=out_shape,
        grid=(b, n_i, n_j),
        in_specs=in_specs,
        out_specs=out_specs,
        scratch_shapes=[pltpu.VMEM((tm, d), BF16), pltpu.VMEM((tm, d), BF16)],
        compiler_params=_cparams(("arbitrary", "arbitrary", "arbitrary")),
        name=name,
    )(h, h, g.reshape(1, d), shift, scale, shift, scale, *[w for w, _, _ in weights])


def _matmul_res_kernel(a_ref, h_ref, gate_ref, *rest, glu):
    a = a_ref[...]
    acc = jnp.dot(a, rest[0][...], preferred_element_type=F32)
    if glu:
        gl = jnp.dot(a, rest[1][...], preferred_element_type=F32)
        half = 0.5 * acc
        acc = half + half * jnp.tanh(0.5 * gl)
    o_ref = rest[-1]
    o_ref[...] = h_ref[...] + gate_ref[...] * acc


def _matmul_res(a, w, layer, h, gate, glu, name):
    b, t, k = a.shape
    n = h.shape[-1]
    tm = min(1024, t)
    tn = 512
    in_specs = [
        pl.BlockSpec((None, tm, k), lambda bb, i, j: (bb, i, 0)),
        pl.BlockSpec((None, tm, tn), lambda bb, i, j: (bb, i, j)),
        pl.BlockSpec((None, 1, tn), lambda bb, i, j: (bb, 0, j)),
        pl.BlockSpec((None, k, tn), functools.partial(_w_index, layer=layer, off=0)),
    ]
    args = [a, h, gate, w]
    if glu:
        in_specs.append(pl.BlockSpec((None, k, tn), functools.partial(_w_index, layer=layer, off=n // tn)))
        args.append(w)
    return pl.pallas_call(
        functools.partial(_matmul_res_kernel, glu=glu),
        out_shape=jax.ShapeDtypeStruct(h.shape, F32),
        grid=(b, t // tm, n // tn),
        in_specs=in_specs,
        out_specs=pl.BlockSpec((None, tm, tn), lambda bb, i, j: (bb, i, j)),
        compiler_params=_cparams(("arbitrary", "arbitrary", "arbitrary")),
        name=name,
    )(*args)


def _gmlp_gate_kernel(u_ref, v_ref, lng_ref, lnb_ref, ws_ref, bs_ref, t_ref):
    v = v_ref[...].astype(F32)
    mu = jnp.mean(v, axis=-1, keepdims=True)
    xc = v - mu
    var = jnp.mean(xc * xc, axis=-1, keepdims=True)
    vn = (xc * lax.rsqrt(var + EPS) * lng_ref[...] + lnb_ref[...]).astype(BF16)
    hd = v.shape[-1] // GM_HEADS
    for cc in range(v.shape[0] // CHUNK):
        rows = slice(cc * CHUNK, (cc + 1) * CHUNK)
        for hh in range(GM_HEADS):
            sl = slice(hh * hd, (hh + 1) * hd)
            s = jnp.dot(ws_ref[hh], vn[rows, sl], preferred_element_type=F32) + bs_ref[hh]
            t_ref[rows, sl] = (u_ref[rows, sl].astype(F32) * s).astype(BF16)


def _gmlp_gate(u, v, ln_g, ln_b, w_s, b_s):
    b, t, e = u.shape
    tr = 2 * CHUNK
    return pl.pallas_call(
        _gmlp_gate_kernel,
        out_shape=jax.ShapeDtypeStruct((b, t, e), BF16),
        grid=(b, t // tr),
        in_specs=[
            pl.BlockSpec((None, tr, e), lambda bb, i: (bb, i, 0)),
            pl.BlockSpec((None, tr, e), lambda bb, i: (bb, i, 0)),
            pl.BlockSpec((1, e), lambda bb, i: (0, 0)),
            pl.BlockSpec((1, e), lambda bb, i: (0, 0)),
            pl.BlockSpec((GM_HEADS, CHUNK, CHUNK), lambda bb, i: (0, 0, 0)),
            pl.BlockSpec((GM_HEADS, CHUNK, 1), lambda bb, i: (0, 0, 0)),
        ],
        out_specs=pl.BlockSpec((None, tr, e), lambda bb, i: (bb, i, 0)),
        compiler_params=_cparams(("arbitrary", "arbitrary")),
        name="gmlp_gate",
    )(u, v, ln_g.reshape(1, e), ln_b.reshape(1, e), w_s.astype(BF16), b_s.reshape(GM_HEADS, CHUNK, 1))


def _dot_split(a, b):
    a_hi, b_hi = a.astype(BF16), b.astype(BF16)
    a_lo = (a - a_hi.astype(F32)).astype(BF16)
    b_lo = (b - b_hi.astype(F32)).astype(BF16)
    dot = functools.partial(jnp.dot, preferred_element_type=F32)
    return dot(a_hi, b_hi) + (dot(a_hi, b_lo) + dot(a_lo, b_hi))


def _s5_ops_kernel(lane_ref, bn_ref, cn_ref, min_ref, mintra_ref, mout_ref, dec_ref):
    t = S5_T
    sw = S5_SW
    mask_b =(lax.broadcasted_iota(jnp.int32, (LANES, sw), 0) // S5_GROUP
              == lax.broadcasted_iota(jnp.int32, (LANES, sw), 1) // S5_STATE)
    mask_c = (lax.broadcasted_iota(jnp.int32, (sw, LANES), 0) // S5_STATE
              == lax.broadcasted_iota(jnp.int32, (sw, LANES), 1) // S5_GROUP)

    n128 = sw // LANES

    def to_wide(x):
        return jnp.concatenate([x[i * LANES:(i + 1) * LANES, :].T for i in range(n128)], axis=1)

    def to_tall(x):
        return jnp.concatenate([x[:, i * LANES:(i + 1) * LANES].T for i in range(n128)], axis=0)

    def to_rows(v):
        return jnp.concatenate(
            [jnp.broadcast_to(v[:, i * LANES:(i + 1) * LANES], (LANES, LANES)).T for i in range(sw // LANES)],
            axis=0)

    lag = []
    for dd in range(2):
        a_re, a_im = lane_ref[dd, 0:1, :], lane_ref[dd, 1:2, :]
        dt = jnp.exp(lane_ref[dd, 2:3, :])
        mag = jnp.exp(a_re * dt)
        p1_re, p1_im = mag * jnp.cos(a_im * dt), mag * jnp.sin(a_im * dt)
        nr, ni = p1_re - 1.0, p1_im
        den = a_re * a_re + a_im * a_im
        cf_re, cf_im = (nr * a_re + ni * a_im) / den, (ni * a_re - nr * a_im) / den
        b_re = to_wide(jnp.where(mask_c, bn_ref[dd, 0], 0.0))
        b_im = to_wide(jnp.where(mask_c, bn_ref[dd, 1], 0.0))
        bb_re = cf_re * b_re - cf_im * b_im
        bb_im = cf_re * b_im + cf_im * b_re
        pw = [(jnp.ones_like(p1_re), jnp.zeros_like(p1_re))]
        for _ in range(t):
            pr, pi = pw[-1]
            pw.append((pr * p1_re - pi * p1_im, pr * p1_im + pi * p1_re))
        dec_ref[dd, 0:1, :] = pw[t][0]
        dec_ref[dd, 1:2, :] = pw[t][1]
        ab = [(pr * bb_re - pi * bb_im, pr * bb_im + pi * bb_re) for pr, pi in pw[:t]]
        for s in range(t):
            m_re, m_im = ab[t - 1 - s] if dd == 0 else ab[s]
            min_ref[dd, s * LANES:(s + 1) * LANES, 0:sw] = m_re.astype(BF16)
            min_ref[dd, s * LANES:(s + 1) * LANES, sw:2 * sw] = m_im.astype(BF16)
        r1_re, r1_im = to_rows(p1_re), to_rows(p1_im)
        qr = to_tall(jnp.where(mask_b, cn_ref[dd, 0], 0.0))
        qi = to_tall(jnp.where(mask_b, cn_ref[dd, 1], 0.0))
        lag.append(_dot_split(jnp.concatenate([x[0] for x in ab], axis=0), qr)
                   - _dot_split(jnp.concatenate([x[1] for x in ab], axis=0), qi))
        mo = []
        for _ in range(t + 1):
            mo.append((qr, -qi))
            qr, qi = qr * r1_re - qi * r1_im, qr * r1_im + qi * r1_re
        for tt in range(t):
            m_re, m_im = mo[tt + 1] if dd == 0 else mo[t - tt]
            mout_ref[dd, 0:sw, tt * LANES:(tt + 1) * LANES] = m_re.astype(BF16)
            mout_ref[dd, sw:2 * sw, tt * LANES:(tt + 1) * LANES] = m_im.astype(BF16)

    def lag_blk(dd, k):
        return lag[dd][k * LANES:(k + 1) * LANES, :]

    for s in range(t):
        for tt in range(t):
            if s < tt:
                blk = lag_blk(0, tt - s)
            elif s > tt:
                blk = lag_blk(1, s - tt)
            else:
                blk = lag_blk(0, 0) + lag_blk(1, 0)
            mintra_ref[s * LANES:(s + 1) * LANES, tt * LANES:(tt + 1) * LANES] = blk.astype(BF16)


def _s5_ops(a_re, a_im, log_dt, b_re, b_im, c_re, c_im):
    nl, _, n_g, p = a_re.shape
    q = b_re.shape[-1]
    n_blk = n_g // S5_GB
    nb = nl * n_blk
    lane = jnp.stack([a_re, a_im, jnp.broadcast_to(log_dt[..., None], a_re.shape)], axis=2)
    lane = jnp.transpose(lane.reshape(nl, 2, 3, n_blk, S5_SW), (0, 3, 1, 2, 4)).reshape(nb, 2, 3, S5_SW)
    bn = jnp.stack([b_re, b_im], axis=2).reshape(nl, 2, 2, n_blk, S5_SW, q)
    bn = jnp.tile(jnp.transpose(bn, (0, 3, 1, 2, 4, 5)), (1, 1, 1, 1, 1, S5_GB)).reshape(nb, 2, 2, S5_SW, LANES)
    cn = jnp.stack([c_re, c_im], axis=2).reshape(nl, 2, 2, n_blk, S5_GB * q, p)
    cn = jnp.tile(jnp.transpose(cn, (0, 3, 1, 2, 4, 5)), (1, 1, 1, 1, 1, S5_GB)).reshape(nb, 2, 2, LANES, S5_SW)
    return pl.pallas_call(
        _s5_ops_kernel,
        out_shape=[
            jax.ShapeDtypeStruct((nb, 2, S5_KW, 2 * S5_SW), BF16),
            jax.ShapeDtypeStruct((nb, S5_KW, S5_KW), BF16),
            jax.ShapeDtypeStruct((nb, 2, 2 * S5_SW, S5_KW), BF16),
            jax.ShapeDtypeStruct((nb, 2, 2, S5_SW), F32),
        ],
        grid=(nb,),
        in_specs=[
            pl.BlockSpec((None, 2, 3, S5_SW), lambda j: (j, 0, 0, 0)),
            pl.BlockSpec((None, 2, 2, S5_SW, LANES), lambda j: (j, 0, 0, 0, 0)),
            pl.BlockSpec((None, 2, 2, LANES, S5_SW), lambda j: (j, 0, 0, 0, 0)),
        ],
        out_specs=[
            pl.BlockSpec((None, 2, S5_KW, 2 * S5_SW), lambda j: (j, 0, 0, 0)),
            pl.BlockSpec((None, S5_KW, S5_KW), lambda j: (j, 0, 0)),
            pl.BlockSpec((None, 2, 2 * S5_SW, S5_KW), lambda j: (j, 0, 0, 0)),
            pl.BlockSpec((None, 2, 2, S5_SW), lambda j: (j, 0, 0, 0)),
        ],
        compiler_params=_cparams(("arbitrary",)),
        name="s5_ops",
    )(lane, bn, cn)


def _s5_kernel(uc_ref, ul_ref, d_ref, min_ref, mintra_ref, mout_ref, dec_ref, zc_ref, zl_ref,
               ucb_ref, xh_ref, y_ref, *, nc_ctx, nc_lat, n_rb):
    nch = nc_ctx + nc_lat
    t = S5_T
    ctx_len = nc_ctx * t
    bounds = [0] + [nc_ctx + (k + 1) * (nc_lat // n_rb) for k in range(n_rb)]

    nsl = 2 * S5_SW // LANES
    ncs = nsl // 2

    def row_pieces(dd, lo, hi):
        if dd == 0:
            return [(lo, hi, lo)]
        out = []
        if lo < nc_ctx:
            out.append((lo, min(hi, nc_ctx), lo + nc_lat))
        if hi > nc_ctx:
            out.append((max(lo, nc_ctx), hi, max(lo, nc_ctx) - nc_ctx))
        return out

    for rb in range(n_rb):
        lo, hi = bounds[rb], bounds[rb + 1]
        for tt in range(t):
            cols = slice(tt * LANES, (tt + 1) * LANES)
            if lo < nc_ctx:
                ucb_ref[lo:nc_ctx, cols] = uc_ref[pl.ds(lo * t + tt, nc_ctx - lo, stride=t), :].astype(BF16)
            l0 = max(lo, nc_ctx)
            ucb_ref[l0:hi, cols] = ul_ref[pl.ds((l0 - nc_ctx) * t + tt, hi - l0, stride=t), :].astype(BF16)
        for dd in range(2):
            x = jnp.dot(ucb_ref[lo:hi, :], min_ref[dd], preferred_element_type=F32)
            for s0, s1, d0 in row_pieces(dd, lo, hi):
                for k in range(nsl):
                    xh_ref[dd, k, d0:d0 + s1 - s0, :] = x[s0 - lo:s1 - lo, k * LANES:(k + 1) * LANES]

    seg = nch // 8
    assert seg % 4 == 0

    def start_row(dd, i):
        return i if dd == 0 else seg - 1 - i

    def dec_slab(dd, k, ri):
        return dec_ref[dd, ri:ri + 1, k * LANES:(k + 1) * LANES]

    dec_b = [[(jnp.broadcast_to(dec_slab(dd, k, 0), (8, LANES)), jnp.broadcast_to(dec_slab(dd, k, 1), (8, LANES)))
              for k in range(ncs)] for dd in range(2)]

    def cmul(a, b):
        return a[0] * b[0] - a[1] * b[1], a[0] * b[1] + a[1] * b[0]

    def cmuladd(a, h, x):
        return a[0] * h[0] - a[1] * h[1] + x[0], a[0] * h[1] + a[1] * h[0] + x[1]

    dec4_b = [[cmul(cmul(a, a), cmul(a, a)) for a in per_dir] for per_dir in dec_b]

    def load_rows(dd, k, i):
        rows = pl.ds(start_row(dd, i), 8, stride=seg)
        return xh_ref.at[dd, k][rows, :], xh_ref.at[dd, ncs + k][rows, :]

    def store_rows(dd, k, i, h):
        rows = pl.ds(start_row(dd, i), 8, stride=seg)
        xh_ref.at[dd, k][rows, :] = h[0]
        xh_ref.at[dd, ncs + k][rows, :] = h[1]

    def segment_ends(i4, st):
        new = []
        for dd in range(2):
            for k in range(ncs):
                a = dec_b[dd][k]
                q = load_rows(dd, k, 4 * i4)
                for m in range(1, 4):
                    q = cmuladd(a, q, load_rows(dd, k, 4 * i4 + m))
                new += list(cmuladd(dec4_b[dd][k], (st[2 * (dd * ncs + k)], st[2 * (dd * ncs + k) + 1]), q))
        return tuple(new)

    zero = jnp.zeros((8, LANES), F32)
    ends = lax.fori_loop(0, seg // 4, segment_ends, (zero,) * (4 * ncs))

    def cpow(a, e):
        acc, sq = None, a
        while e:
            if e & 1:
                acc = sq if acc is None else cmul(acc, sq)
            sq = cmul(sq, sq)
            e >>= 1
        return acc

    carries = []
    for dd in range(2):
        order = list(range(8)) if dd == 0 else list(range(7, -1, -1))
        for k in range(ncs):
            a_seg = cpow((dec_slab(dd, k, 0), dec_slab(dd, k, 1)), seg)
            lr, li = ends[2 * (dd * ncs + k)], ends[2 * (dd * ncs + k) + 1]
            c = (jnp.zeros((1, LANES), F32), jnp.zeros((1, LANES), F32))
            by_sublane = {}
            for m in order:
                by_sublane[m] = c
                pr, pi = cmul(a_seg, c)
                c = (pr + lr[m:m + 1, :], pi + li[m:m + 1, :])
            carries.append(jnp.concatenate([by_sublane[m][0] for m in range(8)], axis=0))
            carries.append(jnp.concatenate([by_sublane[m][1] for m in range(8)], axis=0))

    def scan_rows(i, st):
        new = []
        for dd in range(2):
            for k in range(ncs):
                s0 = (st[2 * (dd * ncs + k)], st[2 * (dd * ncs + k) + 1])
                x0 = load_rows(dd, k, i)
                store_rows(dd, k, i, s0)
                new += list(cmuladd(dec_b[dd][k], s0, x0))
        return tuple(new)

    lax.fori_loop(0, seg, scan_rows, tuple(carries), unroll=2)

    def state_rows(dd, lo, hi):
        parts = [jnp.concatenate([xh_ref[dd, k, d0:d0 + s1 - s0, :] for k in range(nsl)], axis=-1)
                 for s0, s1, d0 in row_pieces(dd, lo, hi)]
        return (parts[0] if len(parts) == 1 else jnp.concatenate(parts, axis=0)).astype(BF16)

    dv = d_ref[...]
    for rb in range(n_rb):
        lo, hi = bounds[rb], bounds[rb + 1]
        y = jnp.dot(ucb_ref[lo:hi, :], mintra_ref[...], preferred_element_type=F32)
        y = y + jnp.dot(state_rows(0, lo, hi), mout_ref[0], preferred_element_type=F32)
        y = y + jnp.dot(state_rows(1, lo, hi), mout_ref[1], preferred_element_type=F32)
        for tt in range(t):
            y_ref[pl.ds(lo * t + tt, hi - lo, stride=t), :] = y[:, tt * LANES:(tt + 1) * LANES]
        if lo < nc_ctx:
            zc_ref[lo * t:ctx_len, :] = jax.nn.gelu(
                (y_ref[lo * t:ctx_len, :] + uc_ref[lo * t:ctx_len, :] * dv).astype(BF16))
        p0, p1 = (max(lo, nc_ctx) - nc_ctx) * t, (hi - nc_ctx) * t
        zl_ref[p0:p1, :] = jax.nn.gelu(
            (y_ref[ctx_len + p0:ctx_len + p1, :] + ul_ref[p0:p1, :] * dv).astype(BF16))


def _s5_core(u_ctx, u_lat, d_skip, ops, layer):
    m_in, m_intra, m_out, decay = ops
    b, l, w = u_lat.shape
    ctx_len = u_ctx.shape[1]
    n_blk = w // LANES
    nc_ctx, nc_lat = ctx_len // S5_T, l // S5_T
    nch = nc_ctx + nc_lat
    n_rb = 4
    assert nc_ctx % 16 == 0 and nc_lat % (16 * n_rb) == 0 and nch % 8 == 0
    kern = functools.partial(_s5_kernel, nc_ctx=nc_ctx, nc_lat=nc_lat, n_rb=n_rb)
    base = layer * n_blk
    return pl.pallas_call(
        kern,
        out_shape=[jax.ShapeDtypeStruct(u_ctx.shape, BF16), jax.ShapeDtypeStruct(u_lat.shape, BF16)],
        grid=(n_blk, b),
        in_specs=[
            pl.BlockSpec((None, ctx_len, LANES), lambda j, bb: (bb, 0, j)),
            pl.BlockSpec((None, l, LANES), lambda j, bb: (bb, 0, j)),
            pl.BlockSpec((None, 1, LANES), lambda j, bb: (layer, 0, j)),
            pl.BlockSpec((None, 2, S5_KW, 2 * S5_SW), lambda j, bb: (base + j, 0, 0, 0)),
            pl.BlockSpec((None, S5_KW, S5_KW), lambda j, bb: (base + j, 0, 0)),
            pl.BlockSpec((None, 2, 2 * S5_SW, S5_KW), lambda j, bb: (base + j, 0, 0, 0)),
            pl.BlockSpec((None, 2, 2, S5_SW), lambda j, bb: (base + j, 0, 0, 0)),
        ],
        out_specs=[
            pl.BlockSpec((None, ctx_len, LANES), lambda j, bb: (bb, 0, j)),
            pl.BlockSpec((None, l, LANES), lambda j, bb: (bb, 0, j)),
        ],
        scratch_shapes=[
            pltpu.VMEM((nch, S5_KW), BF16),
            pltpu.VMEM((2, 2 * S5_SW // LANES, nch, LANES), F32),
            pltpu.VMEM((nch * S5_T, LANES), F32),
        ],
        compiler_params=_cparams(("arbitrary", "arbitrary")),
        name="s5_core",
    )(u_ctx, u_lat, d_skip.reshape(d_skip.shape[0], 1, w), m_in, m_intra, m_out, decay)


def kernel(x, c, ctx, c_ctx, ada_w, ada_b, norm1_g, norm2_g, ffn_w1, ffn_w3, ffn_w2, gm_w_in, gm_ln_g, gm_ln_b, gm_w_s, gm_b_s, gm_w_out, s5_w_in, s5_a_re, s5_a_im, s5_log_dt, s5_b_re, s5_b_im, s5_c_re, s5_c_im, s5_d, s5_w_glu, final_g):
    bsz, seq, d = x.shape
    depth = ada_w.shape[0]
    ffn_hidden = ffn_w1.shape[-1]
    gm_width = gm_w_out.shape[1]
    s5_layers = [i for i in range(depth) if i % N_MIXERS == 1]
    last_s5 = s5_layers[-1] if s5_layers else -1

    cond_rows = jnp.concatenate([c, c_ctx[None, :]], axis=0)
    mod = _adaln_all(cond_rows, ada_w, ada_b)

    def mods(i, stream):
        if stream == "lat":
            m = mod[i, :bsz]
        else:
            m = mod[i, bsz:bsz + 1]
        return [m[:, None, k * d:(k + 1) * d] for k in range(6)]

    w1, w3, w2 = ffn_w1.astype(BF16), ffn_w3.astype(BF16), ffn_w2.astype(BF16)
    gw_in, gw_out = gm_w_in.astype(BF16), gm_w_out.astype(BF16)
    sw_in, sw_glu = s5_w_in.astype(BF16), s5_w_glu.astype(BF16)
    s5_ops = _s5_ops(s5_a_re, s5_a_im, s5_log_dt, s5_b_re, s5_b_im, s5_c_re, s5_c_im) if s5_layers else None

    h = _embed(x)
    ctx_len = ctx.shape[1]
    hc = ctx.reshape(1, bsz * ctx_len, d)

    for i in range(depth):
        ctx_read = i <= last_s5
        ctx_carry = i < last_s5
        j = i // N_MIXERS
        streams = [("lat", h)]
        if ctx_read:
            streams.append(("ctx", hc))
        new = {}
        if i % N_MIXERS == 0:
            for name, hs in streams:
                if name == "ctx" and not ctx_carry:
                    continue
                sh1, sc1, g1, _, _, _ = mods(i, name)
                u, v = _normmod_matmul(hs, norm1_g[i], sh1, sc1, [(gw_in, j, 0), (gw_in, j, gm_width)],
                                       gm_width, "gelu_pair", [BF16, BF16], 512, "gm_in")
                tg = _gmlp_gate(u, v, gm_ln_g[j], gm_ln_b[j], gm_w_s[j], gm_b_s[j])
                new[name] = _matmul_res(tg, gw_out, j, hs, g1, False, "gm_out")
        else:
            us = {}
            for name, hs in streams:
                sh1, sc1, _, _, _, _ = mods(i, name)
                us[name] = _normmod_matmul(hs, norm1_g[i], sh1, sc1, [(sw_in, j, 0)],
                                           d, "plain", [F32], 512, "s5_in")[0]
            z_ctx, z_lat = _s5_core(us["ctx"].reshape(bsz, ctx_len, d), us["lat"], s5_d, s5_ops, j)
            z_ctx = z_ctx.reshape(1, bsz * ctx_len, d)
            new["lat"] = _matmul_res(z_lat, sw_glu, j, h, mods(i, "lat")[2], True, "s5_out")
            if ctx_carry:
                new["ctx"] = _matmul_res(z_ctx, sw_glu, j, hc, mods(i, "ctx")[2], True, "s5_out")
        h = new["lat"]
        if ctx_carry:
            hc = new["ctx"]

        streams = [("lat", h)] + ([("ctx", hc)] if ctx_carry else [])
        for name, hs in streams:
            _, _, _, sh2, sc2, g2 = mods(i, name)
            tf = _normmod_matmul(hs, norm2_g[i], sh2, sc2, [(w1, i, 0), (w3, i, 0)],
                                 ffn_hidden, "swiglu", [BF16], 512, "ffn_in")[0]
            out = _matmul_res(tf, w2, i, hs, g2, False, "ffn_out")
            if name == "lat":
                h = out
            else:
                hc = out

    return _final_norm(h, final_g)
```

```python
import functools
import math

import jax
import jax.numpy as jnp
from jax import lax
from jax.experimental import pallas as pl
from jax.experimental.pallas import tpu as pltpu

F32 = jnp.float32
BF16 = jnp.bfloat16

EPS = 1e-6
GRID_W = 64
CHUNK = 128
GM_HEADS = 16
S5_GROUP = 16
S5_STATE = 64
N_MIXERS = 2

LANES = 128
S5_T = 8
S5_GB = LANES // S5_GROUP
S5_KW = S5_T * LANES
S5_SW = S5_GB * S5_STATE
VMEM_LIMIT = 56 * 1024 * 1024


def _cparams(sem):
    return pltpu.CompilerParams(dimension_semantics=sem, vmem_limit_bytes=VMEM_LIMIT)


def _adaln_kernel(cond_ref, w_ref, b_ref, o_ref, s_ref, *, n_rows, kc):
    x = cond_ref[...]
    s_ref[...] = x * jax.nn.sigmoid(x)
    d, tn = w_ref.shape
    nslab = tn // LANES

    def body(kk, accs):
        k0 = pl.multiple_of(kk * kc, kc)
        new = []
        for r in range(n_rows):
            sb = s_ref[r, pl.ds(k0, kc), :]
            for n in range(nslab):
                w = w_ref[pl.ds(k0, kc), n * LANES:(n + 1) * LANES]
                new.append(accs[r * nslab + n] + (w * sb).reshape(kc // 8, 8, LANES).sum(axis=0))
        return tuple(new)

    init = tuple(jnp.zeros((8, LANES), F32) for _ in range(n_rows * nslab))
    accs = lax.fori_loop(0, d // kc, body, init)
    o_ref[...] = jnp.zeros(o_ref.shape, F32)
    for r in range(n_rows):
        for n in range(nslab):
            o_ref[r:r + 1, n * LANES:(n + 1) * LANES] = (
                jnp.sum(accs[r * nslab + n], axis=0, keepdims=True) + b_ref[:, n * LANES:(n + 1) * LANES])


def _adaln_all(cond_rows, ada_w, ada_b):
    n_rows, d = cond_rows.shape
    depth, _, n6 = ada_w.shape
    tn = 1536
    cond_b =jnp.broadcast_to(cond_rows[:, :, None], (n_rows, d, LANES))
    return pl.pallas_call(
        functools.partial(_adaln_kernel, n_rows=n_rows, kc=64),
        out_shape=jax.ShapeDtypeStruct((depth, 8, n6), F32),
        grid=(depth, n6 // tn),
        in_specs=[
            pl.BlockSpec((n_rows, d, LANES), lambda l, j: (0, 0, 0)),
            pl.BlockSpec((None, d, tn), lambda l, j: (l, 0, j)),
            pl.BlockSpec((None, 1, tn), lambda l, j: (l, 0, j)),
        ],
        out_specs=pl.BlockSpec((None, 8, tn), lambda l, j: (l, 0, j)),
        scratch_shapes=[pltpu.VMEM((n_rows, d, LANES), F32)],
        compiler_params=_cparams(("arbitrary", "arbitrary")),
        name="adaln",
    )(cond_b, ada_w, ada_b.reshape(depth, 1, n6))


def _embed_kernel(x_ref, o_ref, col_ref, *, rows_per_tile):
    d = x_ref.shape[-1]
    q = d // 4
    k = lax.broadcasted_iota(jnp.int32, (1, q), 1).astype(F32)
    omega = jnp.exp(k * (-math.log(10000.0) / q))
    i = pl.program_id(0)

    @pl.when((i == 0) & (pl.program_id(1) == 0))
    def _():
        c = lax.broadcasted_iota(jnp.int32, (GRID_W, 1), 0).astype(F32)
        ang = c * omega
        col_ref[:, :q] = jnp.sin(ang)
        col_ref[:, q:] = jnp.cos(ang)

    r = (i * rows_per_tile + lax.broadcasted_iota(jnp.int32, (rows_per_tile, 1), 0)).astype(F32)
    ang_r = r * omega
    row_pe = jnp.concatenate([jnp.sin(ang_r), jnp.cos(ang_r)], axis=-1)
    for rr in range(rows_per_tile):
        sl = slice(rr * GRID_W, (rr + 1) * GRID_W)
        o_ref[sl, :2 * q] = x_ref[sl, :2 * q] + row_pe[rr:rr + 1, :]
        o_ref[sl, 2 * q:] = x_ref[sl, 2 * q:] + col_ref[...]


def _embed(x):
    b, l, d = x.shape
    rows_per_tile = 8
    tl = rows_per_tile * GRID_W
    return pl.pallas_call(
        functools.partial(_embed_kernel, rows_per_tile=rows_per_tile),
        out_shape=jax.ShapeDtypeStruct(x.shape, F32),
        grid=(l // tl, b),
        in_specs=[pl.BlockSpec((None, tl, d), lambda i, bb: (bb, i, 0))],
        out_specs=pl.BlockSpec((None, tl, d), lambda i, bb: (bb, i, 0)),
        scratch_shapes=[pltpu.VMEM((GRID_W, d // 2), F32)],
        compiler_params=_cparams(("arbitrary", "arbitrary")),
        name="embed",
    )(x)


def _rmsnorm_kernel(h_ref, g_ref, o_ref):
    x = h_ref[...]
    o_ref[...] = x * lax.rsqrt(jnp.mean(x * x, axis=-1, keepdims=True) + EPS) * g_ref[...]


def _final_norm(h, g):
    b, l, d = h.shape
    tm = 512
    return pl.pallas_call(
        _rmsnorm_kernel,
        out_shape=jax.ShapeDtypeStruct(h.shape, F32),
        grid=(b, l // tm),
        in_specs=[pl.BlockSpec((None, tm, d), lambda bb, i: (bb, i, 0)),
                  pl.BlockSpec((1, d), lambda bb, i: (0, 0))],
        out_specs=pl.BlockSpec((None, tm, d), lambda bb, i: (bb, i, 0)),
        compiler_params=_cparams(("arbitrary", "arbitrary")),
        name="final_norm",
    )(h, g.reshape(1, d))


def _w_index(bb, i, j, *, layer, off):
    return (layer, 0, j + off)


def _normmod_matmul_kernel(hf_ref, hp_ref, g_ref, sh_ref, sc_ref, shn_ref, scn_ref, *rest, n_w, mode, n_pieces):
    w_refs = rest[:n_w]
    out_refs = rest[n_w:-2]
    slots = rest[-2:]
    n = pl.program_id(0) * pl.num_programs(1) + pl.program_id(1)
    j = pl.program_id(2)
    g = g_ref[...]

    def normmod(x, sh, sc):
        r = lax.rsqrt(jnp.mean(x * x, axis=-1, keepdims=True) + EPS)
        return (x * r * (g * (1.0 + sc)) + sh).astype(BF16)

    @pl.when((n == 0) & (j == 0))
    def _():
        slots[0][...] = normmod(hf_ref[...], sh_ref[...], sc_ref[...])

    rp = hp_ref.shape[0]
    piece = jnp.minimum(j, n_pieces - 1)

    def step(cur_ref, nxt_ref):
        a = cur_ref[...]
        accs = [jnp.dot(a, w[...], preferred_element_type=F32) for w in w_refs]
        if mode == "plain":
            out_refs[0][...] = accs[0].astype(out_refs[0].dtype)
        elif mode == "gelu_pair":
            out_refs[0][...] = jax.nn.gelu(accs[0].astype(BF16)).astype(out_refs[0].dtype)
            out_refs[1][...] = jax.nn.gelu(accs[1].astype(BF16)).astype(out_refs[1].dtype)
        elif mode == "swiglu":
            half = 0.5 * accs[0]
            prod = half * accs[1]
            out_refs[0][...] = (prod + prod * jnp.tanh(half)).astype(out_refs[0].dtype)
        else:
            raise ValueError(mode)
        nxt_ref[pl.ds(pl.multiple_of(piece * rp, rp), rp), :] = normmod(hp_ref[...], shn_ref[...], scn_ref[...])

    pl.when(n % 2 == 0)(lambda: step(slots[0], slots[1]))
    pl.when(n % 2 == 1)(lambda: step(slots[1], slots[0]))


def _normmod_matmul(h, g, shift, scale, weights, n, mode, out_dtypes, tn, name):
    b, t, d = h.shape
    tm = min(1024, t)
    n_i, n_j = t // tm, n // tn
    n_pieces = min(n_j, 8)
    rp = tm // n_pieces
    assert rp * n_pieces == tm and rp % 16 == 0

    def next_tile(bb, i):
        nn = jnp.minimum(bb * n_i + i + 1, b * n_i - 1)
        return nn // n_i, nn % n_i

    def piece_index(bb, i, j):
        nb, ni = next_tile(bb, i)
        return (nb, ni * n_pieces + jnp.minimum(j, n_pieces - 1), 0)

    def next_mod_index(bb, i, j):
        return (next_tile(bb, i)[0], 0, 0)

    in_specs = [
        pl.BlockSpec((None, tm, d), lambda bb, i, j: (0, 0, 0)),
        pl.BlockSpec((None, rp, d), piece_index),
        pl.BlockSpec((1, d), lambda bb, i, j: (0, 0)),
        pl.BlockSpec((None, 1, d), lambda bb, i, j: (0, 0, 0)),
        pl.BlockSpec((None, 1, d), lambda bb, i, j: (0, 0, 0)),
        pl.BlockSpec((None, 1, d), next_mod_index),
        pl.BlockSpec((None, 1, d), next_mod_index),
    ]
    for _, layer, off in weights:
        in_specs.append(pl.BlockSpec((None, d, tn), functools.partial(_w_index, layer=layer, off=off // tn)))
    out_shape = [jax.ShapeDtypeStruct((b, t, n), dt) for dt in out_dtypes]
    out_specs = [pl.BlockSpec((None, tm, tn), lambda bb, i, j: (bb, i, j)) for _ in out_dtypes]
    return pl.pallas_call(
        functools.partial(_normmod_matmul_kernel, n_w=len(weights), mode=mode, n_pieces=n_pieces),
        out_shape=out_shape,
        grid=(b, n_i, n_j),
        in_specs=in_specs,
        out_specs=out_specs,
        scratch_shapes=[pltpu.VMEM((tm, d), BF16), pltpu.VMEM((tm, d), BF16)],
        compiler_params=_cparams(("arbitrary", "arbitrary", "arbitrary")),
        name=name,
    )(h, h, g.reshape(1, d), shift, scale, shift, scale, *[w for w, _, _ in weights])


def _matmul_res_kernel(a_ref, h_ref, gate_ref, *rest, glu):
    a = a_ref[...]
    acc = jnp.dot(a, rest[0][...], preferred_element_type=F32)
    if glu:
        gl = jnp.dot(a, rest[1][...], preferred_element_type=F32)
        half = 0.5 * acc
        acc = half + half * jnp.tanh(0.5 * gl)
    o_ref = rest[-1]
    o_ref[...] = h_ref[...] + gate_ref[...] * acc


def _matmul_res(a, w, layer, h, gate, glu, name):
    b, t, k = a.shape
    n = h.shape[-1]
    tm = min(1024, t)
    tn = 512
    in_specs = [
        pl.BlockSpec((None, tm, k), lambda bb, i, j: (bb, i, 0)),
        pl.BlockSpec((None, tm, tn), lambda bb, i, j: (bb, i, j)),
        pl.BlockSpec((None, 1, tn), lambda bb, i, j: (bb, 0, j)),
        pl.BlockSpec((None, k, tn), functools.partial(_w_index, layer=layer, off=0)),
    ]
    args = [a, h, gate, w]
    if glu:
        in_specs.append(pl.BlockSpec((None, k, tn), functools.partial(_w_index, layer=layer, off=n // tn)))
        args.append(w)
    return pl.pallas_call(
        functools.partial(_matmul_res_kernel, glu=glu),
        out_shape=jax.ShapeDtypeStruct(h.shape, F32),
        grid=(b, t // tm, n // tn),
        in_specs=in_specs,
        out_specs=pl.BlockSpec((None, tm, tn), lambda bb, i, j: (bb, i, j)),
        compiler_params=_cparams(("arbitrary", "arbitrary", "arbitrary")),
        name=name,
    )(*args)


def _gmlp_gate_kernel(u_ref, v_ref, lng_ref, lnb_ref, ws_ref, bs_ref, t_ref):
    v = v_ref[...].astype(F32)
    mu = jnp.mean(v, axis=-1, keepdims=True)
    xc = v - mu
    var = jnp.mean(xc * xc, axis=-1, keepdims=True)
    vn = (xc * lax.rsqrt(var + EPS) * lng_ref[...] + lnb_ref[...]).astype(BF16)
    hd = v.shape[-1] // GM_HEADS
    for cc in range(v.shape[0] // CHUNK):
        rows = slice(cc * CHUNK, (cc + 1) * CHUNK)
        for hh in range(GM_HEADS):
            sl = slice(hh * hd, (hh + 1) * hd)
            s = jnp.dot(ws_ref[hh], vn[rows, sl], preferred_element_type=F32) + bs_ref[hh]
            t_ref[rows, sl] = (u_ref[rows, sl].astype(F32) * s).astype(BF16)


def _gmlp_gate(u, v, ln_g, ln_b, w_s, b_s):
    b, t, e = u.shape
    tr = 2 * CHUNK
    return pl.pallas_call(
        _gmlp_gate_kernel,
        out_shape=jax.ShapeDtypeStruct((b, t, e), BF16),
        grid=(b, t // tr),
        in_specs=[
            pl.BlockSpec((None, tr, e), lambda bb, i: (bb, i, 0)),
            pl.BlockSpec((None, tr, e), lambda bb, i: (bb, i, 0)),
            pl.BlockSpec((1, e), lambda bb, i: (0, 0)),
            pl.BlockSpec((1, e), lambda bb, i: (0, 0)),
            pl.BlockSpec((GM_HEADS, CHUNK, CHUNK), lambda bb, i: (0, 0, 0)),
            pl.BlockSpec((GM_HEADS, CHUNK, 1), lambda bb, i: (0, 0, 0)),
        ],
        out_specs=pl.BlockSpec((None, tr, e), lambda bb, i: (bb, i, 0)),
        compiler_params=_cparams(("arbitrary", "arbitrary")),
        name="gmlp_gate",
    )(u, v, ln_g.reshape(1, e), ln_b.reshape(1, e), w_s.astype(BF16), b_s.reshape(GM_HEADS, CHUNK, 1))


def _dot_split(a, b):
    a_hi, b_hi = a.astype(BF16), b.astype(BF16)
    a_lo = (a - a_hi.astype(F32)).astype(BF16)
    b_lo = (b - b_hi.astype(F32)).astype(BF16)
    dot = functools.partial(jnp.dot, preferred_element_type=F32)
    return dot(a_hi, b_hi) + (dot(a_hi, b_lo) + dot(a_lo, b_hi))


def _s5_ops_kernel(lane_ref, bn_ref, cn_ref, min_ref, mintra_ref, mout_ref, dec_ref):
    t = S5_T
    sw = S5_SW
    mask_b =(lax.broadcasted_iota(jnp.int32, (LANES, sw), 0) // S5_GROUP
              == lax.broadcasted_iota(jnp.int32, (LANES, sw), 1) // S5_STATE)
    mask_c = (lax.broadcasted_iota(jnp.int32, (sw, LANES), 0) // S5_STATE
              == lax.broadcasted_iota(jnp.int32, (sw, LANES), 1) // S5_GROUP)

    n128 = sw // LANES

    def to_wide(x):
        return jnp.concatenate([x[i * LANES:(i + 1) * LANES, :].T for i in range(n128)], axis=1)

    def to_tall(x):
        return jnp.concatenate([x[:, i * LANES:(i + 1) * LANES].T for i in range(n128)], axis=0)

    def to_rows(v):
        return jnp.concatenate(
            [jnp.broadcast_to(v[:, i * LANES:(i + 1) * LANES], (LANES, LANES)).T for i in range(sw // LANES)],
            axis=0)

    lag = []
    for dd in range(2):
        a_re, a_im = lane_ref[dd, 0:1, :], lane_ref[dd, 1:2, :]
        dt = jnp.exp(lane_ref[dd, 2:3, :])
        mag = jnp.exp(a_re * dt)
        p1_re, p1_im = mag * jnp.cos(a_im * dt), mag * jnp.sin(a_im * dt)
        nr, ni = p1_re - 1.0, p1_im
        den = a_re * a_re + a_im * a_im
        cf_re, cf_im = (nr * a_re + ni * a_im) / den, (ni * a_re - nr * a_im) / den
        b_re = to_wide(jnp.where(mask_c, bn_ref[dd, 0], 0.0))
        b_im = to_wide(jnp.where(mask_c, bn_ref[dd, 1], 0.0))
        bb_re = cf_re * b_re - cf_im * b_im
        bb_im = cf_re * b_im + cf_im * b_re
        pw = [(jnp.ones_like(p1_re), jnp.zeros_like(p1_re))]
        for _ in range(t):
            pr, pi = pw[-1]
            pw.append((pr * p1_re - pi * p1_im, pr * p1_im + pi * p1_re))
        dec_ref[dd, 0:1, :] = pw[t][0]
        dec_ref[dd, 1:2, :] = pw[t][1]
        ab = [(pr * bb_re - pi * bb_im, pr * bb_im + pi * bb_re) for pr, pi in pw[:t]]
        for s in range(t):
            m_re, m_im = ab[t - 1 - s] if dd == 0 else ab[s]
            min_ref[dd, s * LANES:(s + 1) * LANES, 0:sw] = m_re.astype(BF16)
            min_ref[dd, s * LANES:(s + 1) * LANES, sw:2 * sw] = m_im.astype(BF16)
        r1_re, r1_im = to_rows(p1_re), to_rows(p1_im)
        qr = to_tall(jnp.where(mask_b, cn_ref[dd, 0], 0.0))
        qi = to_tall(jnp.where(mask_b, cn_ref[dd, 1], 0.0))
        lag.append(_dot_split(jnp.concatenate([x[0] for x in ab], axis=0), qr)
                   - _dot_split(jnp.concatenate([x[1] for x in ab], axis=0), qi))
        mo = []
        for _ in range(t + 1):
            mo.append((qr, -qi))
            qr, qi = qr * r1_re - qi * r1_im, qr * r1_im + qi * r1_re
        for tt in range(t):
            m_re, m_im = mo[tt + 1] if dd == 0 else mo[t - tt]
            mout_ref[dd, 0:sw, tt * LANES:(tt + 1) * LANES] = m_re.astype(BF16)
            mout_ref[dd, sw:2 * sw, tt * LANES:(tt + 1) * LANES] = m_im.astype(BF16)

    def lag_blk(dd, k):
        return lag[dd][k * LANES:(k + 1) * LANES, :]

    for s in range(t):
        for tt in range(t):
            if s < tt:
                blk = lag_blk(0, tt - s)
            elif s > tt:
                blk = lag_blk(1, s - tt)
            else:
                blk = lag_blk(0, 0) + lag_blk(1, 0)
            mintra_ref[s * LANES:(s + 1) * LANES, tt * LANES:(tt + 1) * LANES] = blk.astype(BF16)


def _s5_ops(a_re, a_im, log_dt, b_re, b_im, c_re, c_im):
    nl, _, n_g, p = a_re.shape
    q = b_re.shape[-1]
    n_blk = n_g // S5_GB
    nb = nl * n_blk
    lane = jnp.stack([a_re, a_im, jnp.broadcast_to(log_dt[..., None], a_re.shape)], axis=2)
    lane = jnp.transpose(lane.reshape(nl, 2, 3, n_blk, S5_SW), (0, 3, 1, 2, 4)).reshape(nb, 2, 3, S5_SW)
    bn = jnp.stack([b_re, b_im], axis=2).reshape(nl, 2, 2, n_blk, S5_SW, q)
    bn = jnp.tile(jnp.transpose(bn, (0, 3, 1, 2, 4, 5)), (1, 1, 1, 1, 1, S5_GB)).reshape(nb, 2, 2, S5_SW, LANES)
    cn = jnp.stack([c_re, c_im], axis=2).reshape(nl, 2, 2, n_blk, S5_GB * q, p)
    cn = jnp.tile(jnp.transpose(cn, (0, 3, 1, 2, 4, 5)), (1, 1, 1, 1, 1, S5_GB)).reshape(nb, 2, 2, LANES, S5_SW)
    return pl.pallas_call(
        _s5_ops_kernel,
        out_shape=[
            jax.ShapeDtypeStruct((nb, 2, S5_KW, 2 * S5_SW), BF16),
            jax.ShapeDtypeStruct((nb, S5_KW, S5_KW), BF16),
            jax.ShapeDtypeStruct((nb, 2, 2 * S5_SW, S5_KW), BF16),
            jax.ShapeDtypeStruct((nb, 2, 2, S5_SW), F32),
        ],
        grid=(nb,),
        in_specs=[
            pl.BlockSpec((None, 2, 3, S5_SW), lambda j: (j, 0, 0, 0)),
            pl.BlockSpec((None, 2, 2, S5_SW, LANES), lambda j: (j, 0, 0, 0, 0)),
            pl.BlockSpec((None, 2, 2, LANES, S5_SW), lambda j: (j, 0, 0, 0, 0)),
        ],
        out_specs=[
            pl.BlockSpec((None, 2, S5_KW, 2 * S5_SW), lambda j: (j, 0, 0, 0)),
            pl.BlockSpec((None, S5_KW, S5_KW), lambda j: (j, 0, 0)),
            pl.BlockSpec((None, 2, 2 * S5_SW, S5_KW), lambda j: (j, 0, 0, 0)),
            pl.BlockSpec((None, 2, 2, S5_SW), lambda j: (j, 0, 0, 0)),
        ],
        compiler_params=_cparams(("arbitrary",)),
        name="s5_ops",
    )(lane, bn, cn)


def _s5_kernel(uc_ref, ul_ref, d_ref, min_ref, mintra_ref, mout_ref, dec_ref, zc_ref, zl_ref,
               ucb_ref, xh_ref, y_ref, *, nc_ctx, nc_lat, n_rb):
    nch = nc_ctx + nc_lat
    t = S5_T
    ctx_len = nc_ctx * t
    bounds = [0] + [nc_ctx + (k + 1) * (nc_lat // n_rb) for k in range(n_rb)]

    nsl = 2 * S5_SW // LANES
    ncs = nsl // 2

    def row_pieces(dd, lo, hi):
        if dd == 0:
            return [(lo, hi, lo)]
        out = []
        if lo < nc_ctx:
            out.append((lo, min(hi, nc_ctx), lo + nc_lat))
        if hi > nc_ctx:
            out.append((max(lo, nc_ctx), hi, max(lo, nc_ctx) - nc_ctx))
        return out

    for rb in range(n_rb):
        lo, hi = bounds[rb], bounds[rb + 1]
        for tt in range(t):
            cols = slice(tt * LANES, (tt + 1) * LANES)
            if lo < nc_ctx:
                ucb_ref[lo:nc_ctx, cols] = uc_ref[pl.ds(lo * t + tt, nc_ctx - lo, stride=t), :].astype(BF16)
            l0 = max(lo, nc_ctx)
            ucb_ref[l0:hi, cols] = ul_ref[pl.ds((l0 - nc_ctx) * t + tt, hi - l0, stride=t), :].astype(BF16)
        for dd in range(2):
            x = jnp.dot(ucb_ref[lo:hi, :], min_ref[dd], preferred_element_type=F32)
            for s0, s1, d0 in row_pieces(dd, lo, hi):
                for k in range(nsl):
                    xh_ref[dd, k, d0:d0 + s1 - s0, :] = x[s0 - lo:s1 - lo, k * LANES:(k + 1) * LANES]

    seg = nch // 8
    assert seg % 4 == 0

    def start_row(dd, i):
        return i if dd == 0 else seg - 1 - i

    def dec_slab(dd, k, ri):
        return dec_ref[dd, ri:ri + 1, k * LANES:(k + 1) * LANES]

    dec_b = [[(jnp.broadcast_to(dec_slab(dd, k, 0), (8, LANES)), jnp.broadcast_to(dec_slab(dd, k, 1), (8, LANES)))
              for k in range(ncs)] for dd in range(2)]

    def cmul(a, b):
        return a[0] * b[0] - a[1] * b[1], a[0] * b[1] + a[1] * b[0]

    def cmuladd(a, h, x):
        return a[0] * h[0] - a[1] * h[1] + x[0], a[0] * h[1] + a[1] * h[0] + x[1]

    dec4_b = [[cmul(cmul(a, a), cmul(a, a)) for a in per_dir] for per_dir in dec_b]

    def load_rows(dd, k, i):
        rows = pl.ds(start_row(dd, i), 8, stride=seg)
        return xh_ref.at[dd, k][rows, :], xh_ref.at[dd, ncs + k][rows, :]

    def store_rows(dd, k, i, h):
        rows = pl.ds(start_row(dd, i), 8, stride=seg)
        xh_ref.at[dd, k][rows, :] = h[0]
        xh_ref.at[dd, ncs + k][rows, :] = h[1]

    def segment_ends(i4, st):
        new = []
        for dd in range(2):
            for k in range(ncs):
                a = dec_b[dd][k]
                q = load_rows(dd, k, 4 * i4)
                for m in range(1, 4):
                    q = cmuladd(a, q, load_rows(dd, k, 4 * i4 + m))
                new += list(cmuladd(dec4_b[dd][k], (st[2 * (dd * ncs + k)], st[2 * (dd * ncs + k) + 1]), q))
        return tuple(new)

    zero = jnp.zeros((8, LANES), F32)
    ends = lax.fori_loop(0, seg // 4, segment_ends, (zero,) * (4 * ncs))

    def cpow(a, e):
        acc, sq = None, a
        while e:
            if e & 1:
                acc = sq if acc is None else cmul(acc, sq)
            sq = cmul(sq, sq)
            e >>= 1
        return acc

    carries = []
    for dd in range(2):
        order = list(range(8)) if dd == 0 else list(range(7, -1, -1))
        for k in range(ncs):
            a_seg = cpow((dec_slab(dd, k, 0), dec_slab(dd, k, 1)), seg)
            lr, li = ends[2 * (dd * ncs + k)], ends[2 * (dd * ncs + k) + 1]
            c = (jnp.zeros((1, LANES), F32), jnp.zeros((1, LANES), F32))
            by_sublane = {}
            for m in order:
                by_sublane[m] = c
                pr, pi = cmul(a_seg, c)
                c = (pr + lr[m:m + 1, :], pi + li[m:m + 1, :])
            carries.append(jnp.concatenate([by_sublane[m][0] for m in range(8)], axis=0))
            carries.append(jnp.concatenate([by_sublane[m][1] for m in range(8)], axis=0))

    def scan_rows(i, st):
        new = []
        for dd in range(2):
            for k in range(ncs):
                s0 = (st[2 * (dd * ncs + k)], st[2 * (dd * ncs + k) + 1])
                x0 = load_rows(dd, k, i)
                store_rows(dd, k, i, s0)
                new += list(cmuladd(dec_b[dd][k], s0, x0))
        return tuple(new)

    lax.fori_loop(0, seg, scan_rows, tuple(carries), unroll=2)

    def state_rows(dd, lo, hi):
        parts = [jnp.concatenate([xh_ref[dd, k, d0:d0 + s1 - s0, :] for k in range(nsl)], axis=-1)
                 for s0, s1, d0 in row_pieces(dd, lo, hi)]
        return (parts[0] if len(parts) == 1 else jnp.concatenate(parts, axis=0)).astype(BF16)

    dv = d_ref[...]
    for rb in range(n_rb):
        lo, hi = bounds[rb], bounds[rb + 1]
        y = jnp.dot(ucb_ref[lo:hi, :], mintra_ref[...], preferred_element_type=F32)
        y = y + jnp.dot(state_rows(0, lo, hi), mout_ref[0], preferred_element_type=F32)
        y = y + jnp.dot(state_rows(1, lo, hi), mout_ref[1], preferred_element_type=F32)
        for tt in range(t):
            y_ref[pl.ds(lo * t + tt, hi - lo, stride=t), :] = y[:, tt * LANES:(tt + 1) * LANES]
        if lo < nc_ctx:
            zc_ref[lo * t:ctx_len, :] = jax.nn.gelu(
                (y_ref[lo * t:ctx_len, :] + uc_ref[lo * t:ctx_len, :] * dv).astype(BF16))
        p0, p1 = (max(lo, nc_ctx) - nc_ctx) * t, (hi - nc_ctx) * t
        zl_ref[p0:p1, :] = jax.nn.gelu(
            (y_ref[ctx_len + p0:ctx_len + p1, :] + ul_ref[p0:p1, :] * dv).astype(BF16))


def _s5_core(u_ctx, u_lat, d_skip, ops, layer):
    m_in, m_intra, m_out, decay = ops
    b, l, w = u_lat.shape
    ctx_len = u_ctx.shape[1]
    n_blk = w // LANES
    nc_ctx, nc_lat = ctx_len // S5_T, l // S5_T
    nch = nc_ctx + nc_lat
    n_rb = 4
    assert nc_ctx % 16 == 0 and nc_lat % (16 * n_rb) == 0 and nch % 8 == 0
    kern = functools.partial(_s5_kernel, nc_ctx=nc_ctx, nc_lat=nc_lat, n_rb=n_rb)
    base = layer * n_blk
    return pl.pallas_call(
        kern,
        out_shape=[jax.ShapeDtypeStruct(u_ctx.shape, BF16), jax.ShapeDtypeStruct(u_lat.shape, BF16)],
        grid=(n_blk, b),
        in_specs=[
            pl.BlockSpec((None, ctx_len, LANES), lambda j, bb: (bb, 0, j)),
            pl.BlockSpec((None, l, LANES), lambda j, bb: (bb, 0, j)),
            pl.BlockSpec((None, 1, LANES), lambda j, bb: (layer, 0, j)),
            pl.BlockSpec((None, 2, S5_KW, 2 * S5_SW), lambda j, bb: (base + j, 0, 0, 0)),
            pl.BlockSpec((None, S5_KW, S5_KW), lambda j, bb: (base + j, 0, 0)),
            pl.BlockSpec((None, 2, 2 * S5_SW, S5_KW), lambda j, bb: (base + j, 0, 0, 0)),
            pl.BlockSpec((None, 2, 2, S5_SW), lambda j, bb: (base + j, 0, 0, 0)),
        ],
        out_specs=[
            pl.BlockSpec((None, ctx_len, LANES), lambda j, bb: (bb, 0, j)),
            pl.BlockSpec((None, l, LANES), lambda j, bb: (bb, 0, j)),
        ],
        scratch_shapes=[
            pltpu.VMEM((nch, S5_KW), BF16),
            pltpu.VMEM((2, 2 * S5_SW // LANES, nch, LANES), F32),
            pltpu.VMEM((nch * S5_T, LANES), F32),
        ],
        compiler_params=_cparams(("arbitrary", "arbitrary")),
        name="s5_core",
    )(u_ctx, u_lat, d_skip.reshape(d_skip.shape[0], 1, w), m_in, m_intra, m_out, decay)


def kernel(x, c, ctx, c_ctx, ada_w, ada_b, norm1_g, norm2_g, ffn_w1, ffn_w3, ffn_w2, gm_w_in, gm_ln_g, gm_ln_b, gm_w_s, gm_b_s, gm_w_out, s5_w_in, s5_a_re, s5_a_im, s5_log_dt, s5_b_re, s5_b_im, s5_c_re, s5_c_im, s5_d, s5_w_glu, final_g):
    bsz, seq, d = x.shape
    depth = ada_w.shape[0]
    ffn_hidden = ffn_w1.shape[-1]
    gm_width = gm_w_out.shape[1]
    s5_layers = [i for i in range(depth) if i % N_MIXERS == 1]
    last_s5 = s5_layers[-1] if s5_layers else -1

    cond_rows = jnp.concatenate([c, c_ctx[None, :]], axis=0)
    mod = _adaln_all(cond_rows, ada_w, ada_b)

    def mods(i, stream):
        if stream == "lat":
            m = mod[i, :bsz]
        else:
            m = mod[i, bsz:bsz + 1]
        return [m[:, None, k * d:(k + 1) * d] for k in range(6)]

    w1, w3, w2 = ffn_w1.astype(BF16), ffn_w3.astype(BF16), ffn_w2.astype(BF16)
    gw_in, gw_out = gm_w_in.astype(BF16), gm_w_out.astype(BF16)
    sw_in, sw_glu = s5_w_in.astype(BF16), s5_w_glu.astype(BF16)
    s5_ops = _s5_ops(s5_a_re, s5_a_im, s5_log_dt, s5_b_re, s5_b_im, s5_c_re, s5_c_im) if s5_layers else None

    h = _embed(x)
    ctx_len = ctx.shape[1]
    hc = ctx.reshape(1, bsz * ctx_len, d)

    for i in range(depth):
        ctx_read = i <= last_s5
        ctx_carry = i < last_s5
        j = i // N_MIXERS
        streams = [("lat", h)]
        if ctx_read:
            streams.append(("ctx", hc))
        new = {}
        if i % N_MIXERS == 0:
            for name, hs in streams:
                if name == "ctx" and not ctx_carry:
                    continue
                sh1, sc1, g1, _, _, _ = mods(i, name)
                u, v = _normmod_matmul(hs, norm1_g[i], sh1, sc1, [(gw_in, j, 0), (gw_in, j, gm_width)],
                                       gm_width, "gelu_pair", [BF16, BF16], 1024, "gm_in")
                tg = _gmlp_gate(u, v, gm_ln_g[j], gm_ln_b[j], gm_w_s[j], gm_b_s[j])
                new[name] = _matmul_res(tg, gw_out, j, hs, g1, False, "gm_out")
        else:
            us = {}
            for name, hs in streams:
                sh1, sc1, _, _, _, _ = mods(i, name)
                us[name] = _normmod_matmul(hs, norm1_g[i], sh1, sc1, [(sw_in, j, 0)],
                                           d, "plain", [F32], 512, "s5_in")[0]
            z_ctx, z_lat = _s5_core(us["ctx"].reshape(bsz, ctx_len, d), us["lat"], s5_d, s5_ops, j)
            z_ctx = z_ctx.reshape(1, bsz * ctx_len, d)
            new["lat"] = _matmul_res(z_lat, sw_glu, j, h, mods(i, "lat")[2], True, "s5_out")
            if ctx_carry:
                new["ctx"] = _matmul_res(z_ctx, sw_glu, j, hc, mods(i, "ctx")[2], True, "s5_out")
        h = new["lat"]
        if ctx_carry:
            hc = new["ctx"]

        streams = [("lat", h)] + ([("ctx", hc)] if ctx_carry else [])
        for name, hs in streams:
            _, _, _, sh2, sc2, g2 = mods(i, name)
            tf = _normmod_matmul(hs, norm2_g[i], sh2, sc2, [(w1, i, 0), (w3, i, 0)],
                                 ffn_hidden, "swiglu", [BF16], 512, "ffn_in")[0]
            out = _matmul_res(tf, w2, i, hs, g2, False, "ffn_out")
            if name == "lat":
                h = out
            else:
                hc = out

    return _final_norm(h, final_g)
```
